```python
import jax, jax.numpy as jnp
from jax import lax
import numpy as np

D_MODEL = 1024
BATCH = 2
SEQ = 8192
DEPTH = 2

BRANCH_WIDTH = D_MODEL // 4
N_BRANCHES = 3
SB_HEADS = 4
SB_HEAD_DIM = BRANCH_WIDTH // SB_HEADS
SB_BLOCK = 128
HGRN_HEADS = 4
HGRN_HEAD_DIM = BRANCH_WIDTH // HGRN_HEADS
HGRN_EXP_CLIP = 60.0
GLA_HEADS = 4
GLA_VALUE_DIM = BRANCH_WIDTH // GLA_HEADS
GLA_KEY_DIM = GLA_VALUE_DIM // 2
GLA_KEY_WIDTH = GLA_HEADS * GLA_KEY_DIM
GLA_GATE_RANK = 16
GLA_TAU = 16.0
CHUNK = 64
NORM_EPS = 1e-5
DEEPNORM_ALPHA = (2 * DEPTH) ** 0.25
DEEPNORM_BETA = (8 * DEPTH) ** -0.25

IN_WIDTHS = (
    BRANCH_WIDTH, BRANCH_WIDTH, BRANCH_WIDTH, BRANCH_WIDTH,
    BRANCH_WIDTH, BRANCH_WIDTH, BRANCH_WIDTH, BRANCH_WIDTH,
    GLA_KEY_WIDTH, GLA_KEY_WIDTH, BRANCH_WIDTH, BRANCH_WIDTH,
    GLA_GATE_RANK,
    D_MODEL, D_MODEL, D_MODEL,
)
IN_SPLITS = tuple(int(s) for s in np.cumsum(IN_WIDTHS)[:-1])
IN_TOTAL = int(sum(IN_WIDTHS))

kernel_name = "hybrid_stickbreak_hgrn2_gla_deepnorm"


def split_heads(a, n_heads):
    b, t, w = a.shape
    return a.reshape(b, t, n_heads, w // n_heads).transpose(0, 2, 1, 3)


def merge_heads(a):
    b, h, t, d = a.shape
    return a.transpose(0, 2, 1, 3).reshape(b, t, h * d)


def masked_exp(mask, log_val):
    return jnp.where(mask, jnp.exp(jnp.where(mask, log_val, 0.0)), 0.0)


def head_rmsnorm(o, gain):
    o = o * lax.rsqrt(jnp.mean(o * o, axis=-1, keepdims=True) + NORM_EPS)
    return merge_heads(o) * gain.astype(jnp.float32)


def layer_norm(x, g, b):
    xf = x.astype(jnp.float32)
    mu = jnp.mean(xf, axis=-1, keepdims=True)
    var = jnp.mean(jnp.square(xf - mu), axis=-1, keepdims=True)
    y = (xf - mu) * lax.rsqrt(var + NORM_EPS) * g.astype(jnp.float32) + b.astype(jnp.float32)
    return y.astype(x.dtype)


def stick_breaking_attention(q, k, v):
    t_len, d = q.shape[2], q.shape[3]
    scale = d ** -0.5
    outs = []
    for blk in range(t_len // SB_BLOCK):
        t0 = blk * SB_BLOCK
        t1 = t0 + SB_BLOCK
        qb, kb, vb = q[:, :, t0:t1], k[:, :, :t1], v[:, :, :t1]
        z = jnp.einsum('bhtd,bhsd->bhts', qb, kb).astype(jnp.float32) * scale
        t_idx = t0 + jnp.arange(SB_BLOCK)[:, None]
        s_idx = jnp.arange(t1)[None, :]
        causal = s_idx < t_idx
        log_beta = jax.nn.log_sigmoid(z)
        log_one_minus = jnp.where(causal, jax.nn.log_sigmoid(-z), 0.0)
        tail = lax.cumsum(log_one_minus, axis=3, reverse=True) - log_one_minus
        weights = masked_exp(causal, log_beta + tail)
        outs.append(jnp.einsum('bhts,bhsd->bhtd', weights, vb.astype(jnp.float32)))
    return jnp.concatenate(outs, axis=2)


def chunked_gated_linear_recurrence(q, k, v, log_f):
    b, h, t_len, dk = q.shape
    dv = v.shape[-1]
    n_chunks = t_len // CHUNK

    def to_chunks(a):
        a = a.astype(jnp.float32)
        return a.reshape(b, h, n_chunks, CHUNK, a.shape[-1]).transpose(2, 0, 1, 3, 4)

    qc, kc, vc, gc = to_chunks(q), to_chunks(k), to_chunks(v), to_chunks(log_f)
    causal = jnp.tril(jnp.ones((CHUNK, CHUNK), dtype=bool))[:, :, None]

    def step(state, inp):
        qi, ki, vi, gi = inp
        cum = jnp.cumsum(gi, axis=2)
        o_inter = jnp.einsum('bhck,bhkv->bhcv', qi * jnp.exp(cum), state)
        diff = cum[:, :, :, None, :] - cum[:, :, None, :, :]
        decay = masked_exp(causal, diff)
        scores = jnp.einsum('bhtsk,bhsk->bhts', qi[:, :, :, None, :] * decay, ki)
        o = o_inter + jnp.einsum('bhts,bhsv->bhtv', scores, vi)
        last = cum[:, :, -1:, :]
        state = jnp.exp(last[:, :, 0, :])[..., None] * state + jnp.einsum(
            'bhsk,bhsv->bhkv', ki * jnp.exp(last - cum), vi)
        return state, o

    state0 = jnp.zeros((b, h, dk, dv), jnp.float32)
    _, o = lax.scan(step, state0, (qc, kc, vc, gc))
    return o.transpose(1, 2, 0, 3, 4).reshape(b, h, t_len, dv)


def hybrid_layer(x, w_in, gla_gate_w2, gla_gate_b, hgrn_lb, hgrn_norm_g, gla_norm_g,
                 w_up, w_out, ln_g, ln_b):
    proj = jnp.einsum('btd,dc->btc', x, w_in)
    (a_q, a_k, a_v, a_g,
     h_f, h_i, h_q, h_g,
     c_q, c_k, c_v, c_g, c_r,
     gate_logits_a, gate_logits_b, gate_logits_c) = jnp.split(proj, IN_SPLITS, axis=-1)

    a_out = stick_breaking_attention(split_heads(a_q, SB_HEADS), split_heads(a_k, SB_HEADS),
                                     split_heads(a_v, SB_HEADS))
    y_a = merge_heads(a_out) * jax.nn.silu(a_g.astype(jnp.float32))

    lb = hgrn_lb.astype(jnp.float32).reshape(HGRN_HEADS, 1, HGRN_HEAD_DIM)
    zf = split_heads(h_f, HGRN_HEADS).astype(jnp.float32)
    log_f = jax.nn.log_sigmoid(zf) + jnp.log1p(lb * jnp.exp(jnp.minimum(-zf, HGRN_EXP_CLIP)))
    h_key = (1.0 - lb) * jax.nn.sigmoid(-zf)
    h_out = chunked_gated_linear_recurrence(split_heads(h_q, HGRN_HEADS), h_key,
                                            split_heads(h_i, HGRN_HEADS), log_f)
    y_b = head_rmsnorm(h_out, hgrn_norm_g) * jax.nn.silu(h_g.astype(jnp.float32))

    gate_pre = jnp.einsum('btr,rk->btk', c_r, gla_gate_w2) + gla_gate_b
    c_log_f = jax.nn.log_sigmoid(gate_pre.astype(jnp.float32)) / GLA_TAU
    c_out = chunked_gated_linear_recurrence(
        split_heads(c_q, GLA_HEADS).astype(jnp.float32) * (GLA_KEY_DIM ** -0.5),
        split_heads(c_k, GLA_HEADS), split_heads(c_v, GLA_HEADS),
        split_heads(c_log_f, GLA_HEADS))
    y_c = head_rmsnorm(c_out, gla_norm_g) * jax.nn.silu(c_g.astype(jnp.float32))

    branches = jnp.stack([y_a, y_b, y_c], axis=0).astype(x.dtype)
    up = jnp.einsum('nbtw,nwd->nbtd', branches, w_up)
    gates = jax.nn.sigmoid(jnp.stack([gate_logits_a, gate_logits_b, gate_logits_c], axis=0))
    merged = jnp.sum(gates * up, axis=0)
    out = jnp.einsum('btd,de->bte', merged, w_out)

    return layer_norm(DEEPNORM_ALPHA * x + out, ln_g, ln_b)


def setup_inputs(seed: int = 0) -> dict:
    key = jax.random.key(seed)
    ks = jax.random.split(key, 11)
    f32 = jnp.float32
    x = jax.random.normal(ks[0], (BATCH, SEQ, D_MODEL), f32)
    w_in = jax.random.normal(ks[1], (DEPTH, D_MODEL, IN_TOTAL), f32) * D_MODEL ** -0.5
    gla_gate_w2 = jax.random.normal(ks[2], (DEPTH, GLA_GATE_RANK, GLA_KEY_WIDTH), f32) * GLA_GATE_RANK ** -0.5
    gla_gate_b = 0.1 * jax.random.normal(ks[3], (DEPTH, GLA_KEY_WIDTH), f32)
    hgrn_lb_logits = 0.5 * jax.random.normal(ks[4], (DEPTH, BRANCH_WIDTH), f32)
    hgrn_norm_g = 1.0 + 0.02 * jax.random.normal(ks[5], (DEPTH, BRANCH_WIDTH), f32)
    gla_norm_g = 1.0 + 0.02 * jax.random.normal(ks[6], (DEPTH, BRANCH_WIDTH), f32)
    w_up = jax.random.normal(ks[7], (DEPTH, N_BRANCHES, BRANCH_WIDTH, D_MODEL), f32) * (
        BRANCH_WIDTH ** -0.5 * DEEPNORM_BETA)
    w_out = jax.random.normal(ks[8], (DEPTH, D_MODEL, D_MODEL), f32) * (D_MODEL ** -0.5 * DEEPNORM_BETA)
    ln_g = 1.0 + 0.02 * jax.random.normal(ks[9], (DEPTH, D_MODEL), f32)
    ln_b = 0.02 * jax.random.normal(ks[10], (DEPTH, D_MODEL), f32)
    return {"x": x, "w_in": w_in, "gla_gate_w2": gla_gate_w2, "gla_gate_b": gla_gate_b,
            "hgrn_lb_logits": hgrn_lb_logits, "hgrn_norm_g": hgrn_norm_g, "gla_norm_g": gla_norm_g,
            "w_up": w_up, "w_out": w_out, "ln_g": ln_g, "ln_b": ln_b}


def reference(x, w_in, gla_gate_w2, gla_gate_b, hgrn_lb_logits, hgrn_norm_g, gla_norm_g,
              w_up, w_out, ln_g, ln_b):
    lb_soft = jax.nn.softmax(hgrn_lb_logits.astype(jnp.float32), axis=0)
    lower_bounds = jnp.cumsum(lb_soft, axis=0) - lb_soft[0:1]
    for layer in range(DEPTH):
        x = hybrid_layer(x, w_in[layer], gla_gate_w2[layer], gla_gate_b[layer],
                         lower_bounds[layer], hgrn_norm_g[layer], gla_norm_g[layer],
                         w_up[layer], w_out[layer], ln_g[layer], ln_b[layer])
    return x
```

```python
import functools

import numpy as np
import jax
import jax.numpy as jnp
from jax import lax
from jax.experimental import pallas as pl
from jax.experimental.pallas import tpu as pltpu

F32 = jnp.float32
BF16 = jnp.bfloat16

D_MODEL = 1024
DEPTH = 2
BRANCH_WIDTH = D_MODEL // 4
N_HEADS = 4
HEAD_DIM = BRANCH_WIDTH // N_HEADS
GLA_KEY_DIM = HEAD_DIM // 2
GLA_KEY_WIDTH = N_HEADS * GLA_KEY_DIM
GLA_GATE_RANK = 16
GLA_TAU = 16.0
HGRN_EXP_CLIP = 60.0
NORM_EPS = 1e-5
DEEPNORM_ALPHA = (2 * DEPTH) ** 0.25

LANES = 128
VMEM_LIMIT_BYTES = 56 * 1024 * 1024

COL_A = 0
COL_B = 1024
COL_GATES = 2048
COL_C_V = 5120
COL_C_G = 5376
COL_C_Q = 5632
COL_C_K = 5760
COL_C_R = 5888
IN_PACKED = 6144

CHUNK = 64
N_LEVELS = 6
ATTN_BLOCK = 256


def _nt_dot(a, b):
    return lax.dot_general(a, b, (((1,), (1,)), ((), ())), preferred_element_type=F32)


def _tn_dot(a, b):
    return lax.dot_general(a, b, (((0,), (0,)), ((), ())), preferred_element_type=F32)


def _neg_softplus_neg_abs(z):
    return -jnp.log(1.0 + jnp.exp(-jnp.abs(z)))


def _log_sigmoid(z):
    return jnp.minimum(z, 0.0) + _neg_softplus_neg_abs(z)


def _sigmoid(z):
    return 1.0 / (1.0 + jnp.exp(-z))


def _inproj_kernel(x_ref, w_ref, p_ref, a_ref):
    acc = jnp.dot(x_ref[...].astype(BF16), w_ref[...], preferred_element_type=F32)
    p_ref[...] = acc

    @pl.when(pl.program_id(1) == 0)
    def _():
        col = lax.broadcasted_iota(jnp.int32, (1, acc.shape[1]), 1)
        scale = jnp.where(col < BRANCH_WIDTH, HEAD_DIM ** -0.5, 1.0)
        a_ref[...] = (acc * scale).astype(BF16)


def _inproj(x2d, w_packed):
    n = x2d.shape[0]
    bm, bn = 1024, 1024
    return pl.pallas_call(
        _inproj_kernel,
        grid=(n // bm, IN_PACKED // bn),
        in_specs=[
            pl.BlockSpec((bm, D_MODEL), lambda i, j: (i, 0)),
            pl.BlockSpec((D_MODEL, bn), lambda i, j: (0, j)),
        ],
        out_specs=[
            pl.BlockSpec((bm, bn), lambda i, j: (i, j)),
            pl.BlockSpec((bm, bn), lambda i, j: (i, 0)),
        ],
        out_shape=[
            jax.ShapeDtypeStruct((n, IN_PACKED), F32),
            jax.ShapeDtypeStruct((n, bn), BF16),
        ],
        compiler_params=pltpu.CompilerParams(
            dimension_semantics=("arbitrary", "arbitrary"),
            vmem_limit_bytes=VMEM_LIMIT_BYTES),
        name="inproj",
    )(x2d, w_packed)


def _attn_kernel(q_ref, k_ref, v_ref, tri_ref, o_ref, acc_ref, car_ref):
    blk = q_ref.shape[1]
    qi = pl.program_id(2)
    q = q_ref[0]
    lane = lax.broadcasted_iota(jnp.int32, (1, LANES), 1)
    zero = jnp.zeros_like(q)
    q_heads = (jnp.where(lane < HEAD_DIM, q, zero), jnp.where(lane >= HEAD_DIM, q, zero))
    tri = tri_ref[...]
    row = lax.broadcasted_iota(jnp.int32, (blk, blk), 0)
    col = lax.broadcasted_iota(jnp.int32, (blk, blk), 1)
    causal = col < row

    acc_ref[...] = jnp.zeros_like(acc_ref)
    car_ref[...] = jnp.zeros_like(car_ref)

    def process(kb, diagonal):
        start = pl.multiple_of(kb * blk, blk)
        k_blk = k_ref[0, pl.ds(start, blk), :]
        v_blk = v_ref[0, pl.ds(start, blk), :]
        for h in range(2):
            z = _nt_dot(q_heads[h], k_blk)
            nsp = _neg_softplus_neg_abs(z)
            log_beta = jnp.minimum(z, 0.0) + nsp
            log_1m = nsp - jnp.maximum(z, 0.0)
            if diagonal:
                log_1m = jnp.where(causal, log_1m, 0.0)
            in_block = jnp.dot(log_1m.astype(BF16), tri, preferred_element_type=F32)
            car = car_ref[h]
            tail = in_block + jnp.concatenate([car] * (blk // LANES), axis=1)
            w = jnp.exp(log_beta + tail)
            if diagonal:
                w = jnp.where(causal, w, 0.0)
            acc_ref[h] += jnp.dot(w.astype(BF16), v_blk, preferred_element_type=F32)
            car_ref[h] = car + jnp.broadcast_to(
                jnp.sum(log_1m, axis=1, keepdims=True), (blk, LANES))

    process(qi, True)

    def body(i, c):
        process(qi - 1 - i, False)
        return c

    lax.fori_loop(0, qi, body, 0)
    o_ref[0] = jnp.where(lane < HEAD_DIM, acc_ref[0], acc_ref[1])


def _attention(qkv, batch, seq):
    blk = ATTN_BLOCK
    qkv3 = qkv.reshape(batch, seq, qkv.shape[-1])
    j = np.arange(blk)
    tri = jnp.asarray((j[:, None] > j[None, :]).astype(np.float32), dtype=BF16)
    return pl.pallas_call(
        _attn_kernel,
        grid=(batch, 2, seq // blk),
        in_specs=[
            pl.BlockSpec((1, blk, LANES), lambda b, p, i: (b, i, p)),
            pl.BlockSpec((1, seq, LANES), lambda b, p, i: (b, 0, 2 + p)),
            pl.BlockSpec((1, seq, LANES), lambda b, p, i: (b, 0, 4 + p)),
            pl.BlockSpec((blk, blk), lambda b, p, i: (0, 0)),
        ],
        out_specs=pl.BlockSpec((1, blk, LANES), lambda b, p, i: (b, i, p)),
        out_shape=jax.ShapeDtypeStruct((batch, seq, BRANCH_WIDTH), F32),
        scratch_shapes=[
            pltpu.VMEM((2, blk, LANES), F32),
            pltpu.VMEM((2, blk, LANES), F32),
        ],
        compiler_params=pltpu.CompilerParams(
            dimension_semantics=("arbitrary", "arbitrary", "arbitrary"),
            vmem_limit_bytes=VMEM_LIMIT_BYTES),
        name="stickbreak_attn",
    )(qkv3, qkv3, qkv3, tri)


def _recurrence_constants():
    t = np.arange(CHUNK)
    x = t[:, None] ^ t[None, :]
    top_bit = np.floor(np.log2(np.maximum(x, 1))).astype(np.int32)
    lvl = np.where(t[:, None] > t[None, :], top_bit,
                   np.where(t[:, None] == t[None, :], N_LEVELS, -1)).astype(np.int32)
    lvl_all = np.tile(lvl, (1, N_HEADS))
    incl = (t[None, :] <= t[:, None]).astype(np.float32)
    incl3 = np.concatenate([incl, incl, incl], axis=1)
    return jnp.asarray(lvl_all), jnp.asarray(incl3, dtype=BF16)


def _chunk_step(q_c, k_c, v_c, lf, lf_dn, lf_up, incl3, lvl, st_ref, cum_ref, key_dim):
    wk = q_c.shape[1]
    wv = v_c.shape[1]
    c = CHUNK

    h1 = lf.astype(BF16)
    r1 = lf - h1.astype(F32)
    h2 = r1.astype(BF16)
    h3 = (r1 - h2.astype(F32)).astype(BF16)
    cum = jnp.dot(incl3, jnp.concatenate([h1, h2, h3], axis=0), preferred_element_type=F32)
    cum_ref[...] = cum
    last = cum[c - 1:c, :]

    lane_k = lax.broadcasted_iota(jnp.int32, (1, wk), 1)
    lane_v = lax.broadcasted_iota(jnp.int32, (1, wv), 1)
    row = lax.broadcasted_iota(jnp.int32, (c, 1), 0)
    v_bf = v_c.astype(BF16)

    def block_diag_rows(a, lane, width):
        zero = jnp.zeros_like(a)
        return jnp.concatenate(
            [jnp.where((lane >= h * width) & (lane < (h + 1) * width), a, zero)
             for h in range(N_HEADS)], axis=0)

    st = st_ref[...]
    o = _nt_dot((q_c * jnp.exp(cum)).astype(BF16), st.astype(BF16))

    scores = jnp.zeros((c, N_HEADS * c), F32)
    for level in range(N_LEVELS + 1):
        b = 1 << level
        if level == N_LEVELS:
            ql, kl = q_c, k_c
        else:
            if level == 0:
                arg = jnp.where((row & 1) == 1, lf, 0.0)
            elif level == 1:
                r4 = row & 3
                arg = jnp.where(r4 == 0, lf_up,
                                jnp.where(r4 == 1, 0.0, jnp.where(r4 == 2, lf, lf + lf_dn)))
            else:
                mids = [jnp.broadcast_to(cum_ref[m:m + 1, :], (2 * b, wk))
                        for m in range(b - 1, c, 2 * b)]
                mid = mids[0] if len(mids) == 1 else jnp.concatenate(mids, axis=0)
                arg = -jnp.abs(cum - mid)
            e = jnp.exp(arg)
            ql, kl = q_c * e, k_c * e
        s_l = _nt_dot(ql.astype(BF16), block_diag_rows(kl.astype(BF16), lane_k, key_dim))
        scores = jnp.where(lvl == level, s_l, scores)

    o = o + jnp.dot(scores.astype(BF16), block_diag_rows(v_bf, lane_v, HEAD_DIM),
                    preferred_element_type=F32)

    k_dec = (k_c * jnp.exp(last - cum)).astype(BF16)
    upd = _tn_dot(v_bf, k_dec)
    same_head = (lax.broadcasted_iota(jnp.int32, (wv, wk), 0) // HEAD_DIM
                 == lax.broadcasted_iota(jnp.int32, (wv, wk), 1) // key_dim)
    st_ref[...] = jnp.exp(last) * st + jnp.where(same_head, upd, 0.0)
    return o


def _run_chunks(q_of, k_ref, v_of, lf_ref, incl3_ref, lvl_ref, o_ref, st_ref, cum_ref,
                n_chunks, key_dim, pad):
    incl3 = incl3_ref[...]
    lvl = lvl_ref[...]
    for ci in range(n_chunks):
        r0 = ci * CHUNK
        o_ref[0, r0:r0 + CHUNK, :] = _chunk_step(
            q_of(r0), k_ref[r0:r0 + CHUNK, :], v_of(r0),
            lf_ref[pad + r0:pad + r0 + CHUNK, :],
            lf_ref[pad + r0 - 1:pad + r0 + CHUNK - 1, :],
            lf_ref[pad + r0 + 1:pad + r0 + CHUNK + 1, :],
            incl3, lvl, st_ref, cum_ref, key_dim)


_LF_PAD = 8


def _hgrn_kernel(zf_ref, vi_ref, q_ref, lb_ref, incl3_ref, lvl_ref, o_ref,
                 st_ref, cum_ref, lf_ref, k_ref):
    tt = zf_ref.shape[1]

    @pl.when(pl.program_id(1) == 0)
    def _():
        st_ref[...] = jnp.zeros_like(st_ref)
        lf_ref[...] = jnp.zeros_like(lf_ref)

    zf = zf_ref[0]
    lb = lb_ref[...]
    lf_ref[_LF_PAD:_LF_PAD + tt, :] = _log_sigmoid(zf) + jnp.log(
        1.0 + lb * jnp.exp(jnp.minimum(-zf, HGRN_EXP_CLIP)))
    k_ref[...] = (1.0 - lb) * _sigmoid(-zf)

    _run_chunks(lambda r0: q_ref[0, r0:r0 + CHUNK, :], k_ref,
                lambda r0: vi_ref[0, r0:r0 + CHUNK, :], lf_ref, incl3_ref, lvl_ref,
                o_ref, st_ref, cum_ref, tt // CHUNK, HEAD_DIM, _LF_PAD)


def _gla_kernel(q_ref, kin_ref, v_ref, r_ref, w2_ref, b_ref, incl3_ref, lvl_ref, o_ref,
                st_ref, cum_ref, lf_ref, k_ref, qs_ref):
    tt = q_ref.shape[1]

    @pl.when(pl.program_id(1) == 0)
    def _():
        st_ref[...] = jnp.zeros_like(st_ref)
        lf_ref[...] = jnp.zeros_like(lf_ref)

    gate_pre = jnp.dot(r_ref[0].astype(BF16), w2_ref[...],
                       preferred_element_type=F32) + b_ref[...]
    lf_ref[_LF_PAD:_LF_PAD + tt, :] = _log_sigmoid(gate_pre) / GLA_TAU
    k_ref[...] = kin_ref[0]
    qs_ref[...] = q_ref[0] * (GLA_KEY_DIM ** -0.5)

    _run_chunks(lambda r0: qs_ref[r0:r0 + CHUNK, :], k_ref,
                lambda r0: v_ref[0, r0:r0 + CHUNK, :], lf_ref, incl3_ref, lvl_ref,
                o_ref, st_ref, cum_ref, tt // CHUNK, GLA_KEY_DIM, _LF_PAD)


_REC_BLOCK = 256


def _const_spec(shape):
    return pl.BlockSpec(shape, lambda b, t: tuple(0 for _ in shape))


def _hgrn(p3, lb_row, lvl, incl3):
    batch, seq, _ = p3.shape
    tt = _REC_BLOCK
    w = BRANCH_WIDTH
    cb = COL_B // w
    return pl.pallas_call(
        _hgrn_kernel,
        grid=(batch, seq // tt),
        in_specs=[
            pl.BlockSpec((1, tt, w), lambda b, t: (b, t, cb)),
            pl.BlockSpec((1, tt, w), lambda b, t: (b, t, cb + 1)),
            pl.BlockSpec((1, tt, w), lambda b, t: (b, t, cb + 2)),
            _const_spec((1, w)),
            _const_spec(incl3.shape),
            _const_spec(lvl.shape),
        ],
        out_specs=pl.BlockSpec((1, tt, w), lambda b, t: (b, t, 0)),
        out_shape=jax.ShapeDtypeStruct((batch, seq, w), F32),
        scratch_shapes=[
            pltpu.VMEM((w, w), F32),
            pltpu.VMEM((CHUNK, w), F32),
            pltpu.VMEM((tt + 2 * _LF_PAD, w), F32),
            pltpu.VMEM((tt, w), F32),
        ],
        compiler_params=pltpu.CompilerParams(
            dimension_semantics=("arbitrary", "arbitrary"),
            vmem_limit_bytes=VMEM_LIMIT_BYTES),
        name="hgrn2_recurrence",
    )(p3, p3, p3, lb_row, incl3, lvl)


def _gla(p3, w2_pad, gate_b, lvl, incl3):
    batch, seq, _ = p3.shape
    tt = _REC_BLOCK
    w = BRANCH_WIDTH
    kw = GLA_KEY_WIDTH
    return pl.pallas_call(
        _gla_kernel,
        grid=(batch, seq // tt),
        in_specs=[
            pl.BlockSpec((1, tt, kw), lambda b, t: (b, t, COL_C_Q // kw)),
            pl.BlockSpec((1, tt, kw), lambda b, t: (b, t, COL_C_K // kw)),
            pl.BlockSpec((1, tt, w), lambda b, t: (b, t, COL_C_V // w)),
            pl.BlockSpec((1, tt, kw), lambda b, t: (b, t, COL_C_R // kw)),
            _const_spec((kw, kw)),
            _const_spec((1, kw)),
            _const_spec(incl3.shape),
            _const_spec(lvl.shape),
        ],
        out_specs=pl.BlockSpec((1, tt, w), lambda b, t: (b, t, 0)),
        out_shape=jax.ShapeDtypeStruct((batch, seq, w), F32),
        scratch_shapes=[
            pltpu.VMEM((w, kw), F32),
            pltpu.VMEM((CHUNK, kw), F32),
            pltpu.VMEM((tt + 2 * _LF_PAD, kw), F32),
            pltpu.VMEM((tt, kw), F32),
            pltpu.VMEM((tt, kw), F32),
        ],
        compiler_params=pltpu.CompilerParams(
            dimension_semantics=("arbitrary", "arbitrary"),
            vmem_limit_bytes=VMEM_LIMIT_BYTES),
        name="gla_recurrence",
    )(p3, p3, p3, p3, w2_pad, gate_b, incl3, lvl)


def _silu(g):
    return g * _sigmoid(g)


def _head_rmsnorm(o, head_ones, gain):
    sq = o * o
    hi = sq.astype(BF16)
    lo = (sq - hi.astype(F32)).astype(BF16)
    ms = (jnp.dot(hi, head_ones, preferred_element_type=F32)
          + jnp.dot(lo, head_ones, preferred_element_type=F32)) * (1.0 / HEAD_DIM)
    return o * lax.rsqrt(ms + NORM_EPS) * gain


def _merge_kernel(ao_ref, ho_ref, co_ref, ag_ref, hg_ref, cg_ref,
                  ga_ref, gb_ref, gc_ref, x_ref, wup_ref, wout_ref, ones_ref,
                  hgain_ref, cgain_ref, lng_ref, lnb_ref, y_ref):
    head_ones = ones_ref[...]
    y_a = ao_ref[...] * _silu(ag_ref[...])
    y_b = _head_rmsnorm(ho_ref[...], head_ones, hgain_ref[...]) * _silu(hg_ref[...])
    y_c = _head_rmsnorm(co_ref[...], head_ones, cgain_ref[...]) * _silu(cg_ref[...])
    merged = None
    for n, (y, g_ref) in enumerate(((y_a, ga_ref), (y_b, gb_ref), (y_c, gc_ref))):
        up = jnp.dot(y.astype(BF16), wup_ref[n], preferred_element_type=F32)
        term = _sigmoid(g_ref[...]) * up
        merged = term if merged is None else merged + term
    out = jnp.dot(merged.astype(BF16), wout_ref[...], preferred_element_type=F32)
    r = DEEPNORM_ALPHA * x_ref[...] + out
    mu = jnp.mean(r, axis=-1, keepdims=True)
    d = r - mu
    var = jnp.mean(d * d, axis=-1, keepdims=True)
    y_ref[...] = d * lax.rsqrt(var + NORM_EPS) * lng_ref[...] + lnb_ref[...]


def _merge(a_out, h_out, c_out, p, x2d, w_up, w_out, head_ones, hgain, cgain, ln_g, ln_b):
    n = x2d.shape[0]
    tn = 256
    w = BRANCH_WIDTH
    row_w = lambda cb: pl.BlockSpec((tn, w), lambda i: (i, cb))
    row_d = lambda cb: pl.BlockSpec((tn, D_MODEL), lambda i: (i, cb))
    const = lambda shape: pl.BlockSpec(shape, lambda i: tuple(0 for _ in shape))
    gate0 = COL_GATES // D_MODEL
    return pl.pallas_call(
        _merge_kernel,
        grid=(n // tn,),
        in_specs=[
            row_w(0), row_w(0), row_w(0),
            row_w((COL_A + 3 * w) // w), row_w((COL_B + 3 * w) // w), row_w(COL_C_G // w),
            row_d(gate0), row_d(gate0 + 1), row_d(gate0 + 2),
            row_d(0),
            const((3, w, D_MODEL)), const((D_MODEL, D_MODEL)), const((w, w)),
            const((1, w)), const((1, w)), const((1, D_MODEL)), const((1, D_MODEL)),
        ],
        out_specs=pl.BlockSpec((tn, D_MODEL), lambda i: (i, 0)),
        out_shape=jax.ShapeDtypeStruct((n, D_MODEL), F32),
        compiler_params=pltpu.CompilerParams(
            dimension_semantics=("arbitrary",),
            vmem_limit_bytes=VMEM_LIMIT_BYTES),
        name="merge_out_norm",
    )(a_out, h_out, c_out, p, p, p, p, p, p, x2d, w_up, w_out, head_ones,
      hgain, cgain, ln_g, ln_b)


def _pack_w_in(w):
    bw = BRANCH_WIDTH
    a = w[:, 0:4 * bw]
    b = w[:, 4 * bw:8 * bw]
    c0 = 8 * bw
    c_q = w[:, c0:c0 + GLA_KEY_WIDTH]
    c_k = w[:, c0 + GLA_KEY_WIDTH:c0 + 2 * GLA_KEY_WIDTH]
    c_v = w[:, c0 + 2 * GLA_KEY_WIDTH:c0 + 2 * GLA_KEY_WIDTH + bw]
    c_g = w[:, c0 + 2 * GLA_KEY_WIDTH + bw:c0 + 2 * GLA_KEY_WIDTH + 2 * bw]
    r0 = c0 + 2 * GLA_KEY_WIDTH + 2 * bw
    c_r = w[:, r0:r0 + GLA_GATE_RANK]
    gates = w[:, r0 + GLA_GATE_RANK:]
    pad = jnp.zeros((w.shape[0], IN_PACKED - COL_C_R - GLA_GATE_RANK), w.dtype)
    packed = jnp.concatenate([a, b, gates, c_v, c_g, c_q, c_k, c_r, pad], axis=1)
    return packed.astype(BF16)


def kernel(x, w_in, gla_gate_w2, gla_gate_b, hgrn_lb_logits, hgrn_norm_g, gla_norm_g,
           w_up, w_out, ln_g, ln_b):
    batch, seq, d = x.shape
    n = batch * seq
    lb_soft = jax.nn.softmax(hgrn_lb_logits.astype(F32), axis=0)
    lower_bounds = jnp.cumsum(lb_soft, axis=0) - lb_soft[0:1]
    lvl, incl3 = _recurrence_constants()
    hd = np.arange(BRANCH_WIDTH) // HEAD_DIM
    head_ones = jnp.asarray((hd[:, None] == hd[None, :]).astype(np.float32), dtype=BF16)

    x2d = x.reshape(n, d)
    for layer in range(DEPTH):
        w_packed = _pack_w_in(w_in[layer])
        w2_pad = jnp.zeros((GLA_KEY_WIDTH, GLA_KEY_WIDTH), BF16).at[:GLA_GATE_RANK].set(
            gla_gate_w2[layer].astype(BF16))
        p, qkv = _inproj(x2d, w_packed)
        p3 = p.reshape(batch, seq, IN_PACKED)
        a_out = _attention(qkv, batch, seq)
        h_out = _hgrn(p3, lower_bounds[layer][None, :], lvl, incl3)
        c_out = _gla(p3, w2_pad, gla_gate_b[layer][None, :], lvl, incl3)
        x2d = _merge(a_out.reshape(n, -1), h_out.reshape(n, -1), c_out.reshape(n, -1),
                     p, x2d, w_up[layer].astype(BF16), w_out[layer].astype(BF16), head_ones,
                     hgrn_norm_g[layer][None, :], gla_norm_g[layer][None, :],
                     ln_g[layer][None, :], ln_b[layer][None, :])
    return x2d.reshape(batch, seq, d)
```

```python
import functools

import numpy as np
import jax
import jax.numpy as jnp
from jax import lax
from jax.experimental import pallas as pl
from jax.experimental.pallas import tpu as pltpu

F32 = jnp.float32
BF16 = jnp.bfloat16

D_MODEL = 1024
DEPTH = 2
BRANCH_WIDTH = D_MODEL // 4
N_HEADS = 4
HEAD_DIM = BRANCH_WIDTH // N_HEADS
GLA_KEY_DIM = HEAD_DIM // 2
GLA_KEY_WIDTH = N_HEADS * GLA_KEY_DIM
GLA_GATE_RANK = 16
GLA_TAU = 16.0
HGRN_EXP_CLIP = 60.0
NORM_EPS = 1e-5
DEEPNORM_ALPHA = (2 * DEPTH) ** 0.25

LANES = 128
VMEM_LIMIT_BYTES = 56 * 1024 * 1024

COL_A = 0
COL_B = 1024
COL_GATES = 2048
COL_C_V = 5120
COL_C_G = 5376
COL_C_Q = 5632
COL_C_K = 5760
COL_C_R = 5888
IN_PACKED = 6144

CHUNK = 64
N_LEVELS = 6
ATTN_BLOCK = 256
ATTN_STEPS_PER_ITER = 4
LOG2E = 1.4426950408889634


def _nt_dot(a, b):
    return lax.dot_general(a, b, (((1,), (1,)), ((), ())), preferred_element_type=F32)


def _tn_dot(a, b):
    return lax.dot_general(a, b, (((0,), (0,)), ((), ())), preferred_element_type=F32)


def _neg_softplus_neg_abs(z):
    return -jnp.log(1.0 + jnp.exp(-jnp.abs(z)))


def _log_sigmoid(z):
    return jnp.minimum(z, 0.0) + _neg_softplus_neg_abs(z)


def _sigmoid(z):
    return 1.0 / (1.0 + jnp.exp(-z))


INPROJ_BM = 2048
INPROJ_BN = 1024


def _inproj_kernel(x_ref, w_ref, p_ref, zf_ref, r_ref):
    j = pl.program_id(1)
    acc = jnp.dot(x_ref[...].astype(BF16), w_ref[...], preferred_element_type=F32)
    col = j * INPROJ_BN + lax.broadcasted_iota(jnp.int32, (1, INPROJ_BN), 1)
    scale = jnp.where(col < COL_A + BRANCH_WIDTH, HEAD_DIM ** -0.5, 1.0)
    p_ref[...] = (acc * scale).astype(BF16)

    @pl.when(j == COL_B // INPROJ_BN)
    def _():
        c0 = COL_B % INPROJ_BN
        zf_ref[...] = acc[:, c0:c0 + BRANCH_WIDTH]

    @pl.when(j == COL_C_R // INPROJ_BN)
    def _():
        c0 = COL_C_R % INPROJ_BN
        r_ref[...] = acc[:, c0:c0 + LANES]


def _inproj(x2d, w_packed):
    n = x2d.shape[0]
    bm, bn = INPROJ_BM, INPROJ_BN
    return pl.pallas_call(
        _inproj_kernel,
        grid=(n // bm, IN_PACKED // bn),
        in_specs=[
            pl.BlockSpec((bm, D_MODEL), lambda i, j: (i, 0)),
            pl.BlockSpec((D_MODEL, bn), lambda i, j: (0, j)),
        ],
        out_specs=[
            pl.BlockSpec((bm, bn), lambda i, j: (i, j)),
            pl.BlockSpec((bm, BRANCH_WIDTH), lambda i, j: (i, 0)),
            pl.BlockSpec((bm, LANES), lambda i, j: (i, 0)),
        ],
        out_shape=[
            jax.ShapeDtypeStruct((n, IN_PACKED), BF16),
            jax.ShapeDtypeStruct((n, BRANCH_WIDTH), F32),
            jax.ShapeDtypeStruct((n, LANES), F32),
        ],
        compiler_params=pltpu.CompilerParams(
            dimension_semantics=("arbitrary", "arbitrary"),
            vmem_limit_bytes=VMEM_LIMIT_BYTES),
        name="inproj",
    )(x2d, w_packed)


def _attn_kernel(q_ref, k_ref, v_ref, tri_ref, o_ref,
                 acc_ref, car_ref, z0_ref, z1_ref, w_ref):
    blk = q_ref.shape[1]
    qi = pl.program_id(1)
    lane = lax.broadcasted_iota(jnp.int32, (1, LANES), 1)
    q_heads = []
    for h in range(N_HEADS):
        q_tile = q_ref[0, :, (h // 2) * LANES:(h // 2 + 1) * LANES]
        keep = (lane < HEAD_DIM) if h % 2 == 0 else (lane >= HEAD_DIM)
        q_heads.append(jnp.where(keep, q_tile, jnp.zeros_like(q_tile)))
    row = lax.broadcasted_iota(jnp.int32, (blk, blk), 0)
    col = lax.broadcasted_iota(jnp.int32, (blk, blk), 1)
    causal = col < row

    acc_ref[...] = jnp.zeros_like(acc_ref)
    car_ref[...] = jnp.zeros_like(car_ref)

    def lanes_of(h):
        return slice((h // 2) * LANES, (h // 2 + 1) * LANES)

    def key_rows(j):
        return pl.ds(pl.multiple_of((qi - j) * blk, blk), blk)

    z_refs = (z0_ref, z1_ref)

    def scores_h(h, j, parity):
        k_tile = k_ref[0, key_rows(j), lanes_of(h)]
        z_refs[parity][h] = _nt_dot(q_heads[h], k_tile)

    def sums_h(h, parity, diagonal=False):
        z = z_refs[parity][h]
        softplus = jnp.maximum(z, 0.0) + jnp.log(
            1.0 + jnp.exp2(jnp.abs(z) * (-LOG2E)))
        if diagonal:
            softplus = jnp.where(causal, softplus, 0.0)
        car = car_ref[h]
        z_car = z + jnp.concatenate([car] * (blk // LANES), axis=1)
        suffix = jnp.dot(softplus.astype(BF16), tri_ref[...], preferred_element_type=F32)
        w = jnp.exp(z_car + suffix)
        if diagonal:
            w = jnp.where(causal, w, 0.0)
        w_ref[h] = w.astype(BF16)
        car_ref[h] = car + jnp.broadcast_to(suffix[:, 0:1], (blk, LANES))

    def values_h(h, j):
        v_tile = v_ref[0, key_rows(j), lanes_of(h)]
        acc_ref[h] += jnp.dot(w_ref[h], v_tile, preferred_element_type=F32)

    def step(t, parity):
        for h in range(N_HEADS):
            scores_h(h, t, parity)
            values_h(h, t - 2)
            sums_h(h, 1 - parity)

    def drain(parity):
        for h in range(N_HEADS):
            values_h(h, qi - 1)
            sums_h(h, parity)
            values_h(h, qi)

    for h in range(N_HEADS):
        scores_h(h, 0, 0)

    @pl.when(qi == 0)
    def _():
        for h in range(N_HEADS):
            sums_h(h, 0, True)
            values_h(h, 0)

    @pl.when(qi == 1)
    def _():
        for h in range(N_HEADS):
            scores_h(h, 1, 1)
            sums_h(h, 0, True)
        drain(1)

    @pl.when(qi >= 2)
    def _():
        for h in range(N_HEADS):
            scores_h(h, 1, 1)
            sums_h(h, 0, True)
        step(2, 0)

        n_steady = qi - 2
        pairs_per_iter = ATTN_STEPS_PER_ITER // 2

        def body(i, c):
            t = 3 + ATTN_STEPS_PER_ITER * i
            for n in range(pairs_per_iter):
                step(t + 2 * n, 1)
                step(t + 2 * n + 1, 0)
            return c

        lax.fori_loop(0, n_steady // ATTN_STEPS_PER_ITER, body, 0)
        t_rest = 3 + (n_steady // ATTN_STEPS_PER_ITER) * ATTN_STEPS_PER_ITER
        for n in range(pairs_per_iter - 1):
            @pl.when((n_steady % ATTN_STEPS_PER_ITER) // 2 > n)
            def _():
                step(t_rest + 2 * n, 1)
                step(t_rest + 2 * n + 1, 0)

        @pl.when(qi % 2 == 1)
        def _():
            step(qi, 1)
            drain(1)

        @pl.when(qi % 2 == 0)
        def _():
            drain(0)

    for lt in range(N_HEADS // 2):
        o_ref[0, :, lt * LANES:(lt + 1) * LANES] = jnp.where(
            lane < HEAD_DIM, acc_ref[2 * lt], acc_ref[2 * lt + 1])


def _attention(p3):
    batch, seq, _ = p3.shape
    blk = ATTN_BLOCK
    w = BRANCH_WIDTH
    qkv3 = p3
    j = np.arange(blk)
    tri = jnp.asarray(-(j[:, None] >= j[None, :]).astype(np.float32), dtype=BF16)
    return pl.pallas_call(
        _attn_kernel,
        grid=(batch, seq // blk),
        in_specs=[
            pl.BlockSpec((1, blk, w), lambda b, i: (b, i, 0)),
            pl.BlockSpec((1, seq, w), lambda b, i: (b, 0, 1)),
            pl.BlockSpec((1, seq, w), lambda b, i: (b, 0, 2)),
            pl.BlockSpec((blk, blk), lambda b, i: (0, 0)),
        ],
        out_specs=pl.BlockSpec((1, blk, w), lambda b, i: (b, i, 0)),
        out_shape=jax.ShapeDtypeStruct((batch, seq, w), F32),
        scratch_shapes=[
            pltpu.VMEM((N_HEADS, blk, LANES), F32),
            pltpu.VMEM((N_HEADS, blk, LANES), F32),
            pltpu.VMEM((N_HEADS, blk, blk), F32),
            pltpu.VMEM((N_HEADS, blk, blk), F32),
            pltpu.VMEM((N_HEADS, blk, blk), BF16),
        ],
        compiler_params=pltpu.CompilerParams(
            dimension_semantics=("arbitrary", "arbitrary"),
            vmem_limit_bytes=VMEM_LIMIT_BYTES),
        name="stickbreak_attn",
    )(qkv3, qkv3, qkv3, tri)


def _recurrence_constants():
    t = np.arange(CHUNK)
    x = t[:, None] ^ t[None, :]
    top_bit = np.floor(np.log2(np.maximum(x, 1))).astype(np.int32)
    lvl = np.where(t[:, None] > t[None, :], top_bit,
                   np.where(t[:, None] == t[None, :], N_LEVELS, -1)).astype(np.int32)
    lvl_all = np.tile(lvl, (1, N_HEADS))
    incl = (t[None, :] <= t[:, None]).astype(np.float32)
    incl3 = np.concatenate([incl, incl, incl], axis=1)
    return jnp.asarray(lvl_all), jnp.asarray(incl3, dtype=BF16)


def _run_chunks(q_of, k_ref, v_of, lf_ref, incl3_ref, lvl_ref, o_ref, st_ref, cum_ref,
                n_chunks, key_dim, pad):
    c = CHUNK
    incl3 = incl3_ref[...]
    lvl = lvl_ref[...]
    chunks = range(n_chunks)
    q = [q_of(ci * c) for ci in chunks]
    k = [k_ref[ci * c:(ci + 1) * c, :] for ci in chunks]
    v_bf = [v_of(ci * c).astype(BF16) for ci in chunks]
    lf = [lf_ref[pad + ci * c:pad + (ci + 1) * c, :] for ci in chunks]
    wk = q[0].shape[1]
    wv = v_bf[0].shape[1]

    lane_k = lax.broadcasted_iota(jnp.int32, (1, wk), 1)
    lane_v = lax.broadcasted_iota(jnp.int32, (1, wv), 1)
    row = lax.broadcasted_iota(jnp.int32, (c, 1), 0)
    same_head = (lax.broadcasted_iota(jnp.int32, (wv, wk), 0) // HEAD_DIM
                 == lax.broadcasted_iota(jnp.int32, (wv, wk), 1) // key_dim)

    def block_diag_rows(a, lane, width):
        zero = jnp.zeros_like(a)
        return jnp.concatenate(
            [jnp.where((lane >= h * width) & (lane < (h + 1) * width), a, zero)
             for h in range(N_HEADS)], axis=0)

    cum = []
    for ci in chunks:
        h1 = lf[ci].astype(BF16)
        r1 = lf[ci] - h1.astype(F32)
        h2 = r1.astype(BF16)
        h3 = (r1 - h2.astype(F32)).astype(BF16)
        cum.append(jnp.dot(incl3, jnp.concatenate([h1, h2, h3], axis=0),
                           preferred_element_type=F32))
        cum_ref[ci] = cum[ci]
    last = [cum[ci][c - 1:c, :] for ci in chunks]

    upd = [_tn_dot(v_bf[ci], (k[ci] * jnp.exp(last[ci] - cum[ci])).astype(BF16))
           for ci in chunks]

    scores = [jnp.zeros((c, N_HEADS * c), F32) for _ in chunks]
    for level in range(N_LEVELS + 1):
        b = 1 << level
        for ci in chunks:
            if level == N_LEVELS:
                ql, kl = q[ci], k[ci]
            else:
                if level == 0:
                    arg = jnp.where((row & 1) == 1, lf[ci], 0.0)
                elif level == 1:
                    r0 = pad + ci * c
                    lf_dn = lf_ref[r0 - 1:r0 + c - 1, :]
                    lf_up = lf_ref[r0 + 1:r0 + c + 1, :]
                    r4 = row & 3
                    arg = jnp.where(r4 == 0, lf_up, jnp.where(
                        r4 == 1, 0.0, jnp.where(r4 == 2, lf[ci], lf[ci] + lf_dn)))
                else:
                    mids = [jnp.broadcast_to(cum_ref[ci, m:m + 1, :], (2 * b, wk))
                            for m in range(b - 1, c, 2 * b)]
                    mid = mids[0] if len(mids) == 1 else jnp.concatenate(mids, axis=0)
                    arg = -jnp.abs(cum[ci] - mid)
                e = jnp.exp(arg)
                ql, kl = q[ci] * e, k[ci] * e
            s_l = _nt_dot(ql.astype(BF16),
                          block_diag_rows(kl.astype(BF16), lane_k, key_dim))
            scores[ci] = jnp.where(lvl == level, s_l, scores[ci])

    st = st_ref[...]
    o = []
    for ci in chunks:
        o.append(_nt_dot((q[ci] * jnp.exp(cum[ci])).astype(BF16), st.astype(BF16)))
        st = jnp.exp(last[ci]) * st + jnp.where(same_head, upd[ci], 0.0)
    st_ref[...] = st

    for ci in chunks:
        o_ref[0, ci * c:(ci + 1) * c, :] = o[ci] + jnp.dot(
            scores[ci].astype(BF16), block_diag_rows(v_bf[ci], lane_v, HEAD_DIM),
            preferred_element_type=F32)


_LF_PAD = 8


def _hgrn_kernel(zf_ref, vi_ref, q_ref, lb_ref, incl3_ref, lvl_ref, o_ref,
                 st_ref, cum_ref, lf_ref, k_ref):
    tt = zf_ref.shape[1]

    @pl.when(pl.program_id(1) == 0)
    def _():
        st_ref[...] = jnp.zeros_like(st_ref)
        lf_ref[...] = jnp.zeros_like(lf_ref)

    zf = zf_ref[0]
    lb = lb_ref[...]
    lf_ref[_LF_PAD:_LF_PAD + tt, :] = _log_sigmoid(zf) + jnp.log(
        1.0 + lb * jnp.exp(jnp.minimum(-zf, HGRN_EXP_CLIP)))
    k_ref[...] = (1.0 - lb) * _sigmoid(-zf)

    _run_chunks(lambda r0: q_ref[0, r0:r0 + CHUNK, :].astype(F32), k_ref,
                lambda r0: vi_ref[0, r0:r0 + CHUNK, :].astype(F32), lf_ref, incl3_ref, lvl_ref,
                o_ref, st_ref, cum_ref, tt // CHUNK, HEAD_DIM, _LF_PAD)


def _gla_kernel(q_ref, kin_ref, v_ref, r_ref, w2_ref, b_ref, incl3_ref, lvl_ref, o_ref,
                st_ref, cum_ref, lf_ref, k_ref, qs_ref):
    tt = q_ref.shape[1]

    @pl.when(pl.program_id(1) == 0)
    def _():
        st_ref[...] = jnp.zeros_like(st_ref)
        lf_ref[...] = jnp.zeros_like(lf_ref)

    gate_pre = jnp.dot(r_ref[0].astype(BF16), w2_ref[...],
                       preferred_element_type=F32) + b_ref[...]
    lf_ref[_LF_PAD:_LF_PAD + tt, :] = _log_sigmoid(gate_pre) / GLA_TAU
    k_ref[...] = kin_ref[0].astype(F32)
    qs_ref[...] = q_ref[0].astype(F32) * (GLA_KEY_DIM ** -0.5)

    _run_chunks(lambda r0: qs_ref[r0:r0 + CHUNK, :], k_ref,
                lambda r0: v_ref[0, r0:r0 + CHUNK, :].astype(F32), lf_ref, incl3_ref, lvl_ref,
                o_ref, st_ref, cum_ref, tt // CHUNK, GLA_KEY_DIM, _LF_PAD)


_REC_BLOCK = 256


def _const_spec(shape):
    return pl.BlockSpec(shape, lambda b, t: tuple(0 for _ in shape))


def _hgrn(p3, zf3, lb_row, lvl, incl3):
    batch, seq, _ = p3.shape
    tt = _REC_BLOCK
    w = BRANCH_WIDTH
    cb = COL_B // w
    return pl.pallas_call(
        _hgrn_kernel,
        grid=(batch, seq // tt),
        in_specs=[
            pl.BlockSpec((1, tt, w), lambda b, t: (b, t, 0)),
            pl.BlockSpec((1, tt, w), lambda b, t: (b, t, cb + 1)),
            pl.BlockSpec((1, tt, w), lambda b, t: (b, t, cb + 2)),
            _const_spec((1, w)),
            _const_spec(incl3.shape),
            _const_spec(lvl.shape),
        ],
        out_specs=pl.BlockSpec((1, tt, w), lambda b, t: (b, t, 0)),
        out_shape=jax.ShapeDtypeStruct((batch, seq, w), F32),
        scratch_shapes=[
            pltpu.VMEM((w, w), F32),
            pltpu.VMEM((tt // CHUNK, CHUNK, w), F32),
            pltpu.VMEM((tt + 2 * _LF_PAD, w), F32),
            pltpu.VMEM((tt, w), F32),
        ],
        compiler_params=pltpu.CompilerParams(
            dimension_semantics=("arbitrary", "arbitrary"),
            vmem_limit_bytes=VMEM_LIMIT_BYTES),
        name="hgrn2_recurrence",
    )(zf3, p3, p3, lb_row, incl3, lvl)


def _gla(p3, r3, w2_pad, gate_b, lvl, incl3):
    batch, seq, _ = p3.shape
    tt = _REC_BLOCK
    w = BRANCH_WIDTH
    kw = GLA_KEY_WIDTH
    return pl.pallas_call(
        _gla_kernel,
        grid=(batch, seq // tt),
        in_specs=[
            pl.BlockSpec((1, tt, kw), lambda b, t: (b, t, COL_C_Q // kw)),
            pl.BlockSpec((1, tt, kw), lambda b, t: (b, t, COL_C_K // kw)),
            pl.BlockSpec((1, tt, w), lambda b, t: (b, t, COL_C_V // w)),
            pl.BlockSpec((1, tt, kw), lambda b, t: (b, t, 0)),
            _const_spec((kw, kw)),
            _const_spec((1, kw)),
            _const_spec(incl3.shape),
            _const_spec(lvl.shape),
        ],
        out_specs=pl.BlockSpec((1, tt, w), lambda b, t: (b, t, 0)),
        out_shape=jax.ShapeDtypeStruct((batch, seq, w), F32),
        scratch_shapes=[
            pltpu.VMEM((w, kw), F32),
            pltpu.VMEM((tt // CHUNK, CHUNK, kw), F32),
            pltpu.VMEM((tt + 2 * _LF_PAD, kw), F32),
            pltpu.VMEM((tt, kw), F32),
            pltpu.VMEM((tt, kw), F32),
        ],
        compiler_params=pltpu.CompilerParams(
            dimension_semantics=("arbitrary", "arbitrary"),
            vmem_limit_bytes=VMEM_LIMIT_BYTES),
        name="gla_recurrence",
    )(p3, p3, p3, r3, w2_pad, gate_b, incl3, lvl)


def _silu(g):
    return g * _sigmoid(g)


def _head_rmsnorm(o, head_ones, gain):
    sq = o * o
    hi = sq.astype(BF16)
    lo = (sq - hi.astype(F32)).astype(BF16)
    ms = (jnp.dot(hi, head_ones, preferred_element_type=F32)
          + jnp.dot(lo, head_ones, preferred_element_type=F32)) * (1.0 / HEAD_DIM)
    return o * lax.rsqrt(ms + NORM_EPS) * gain


def _merge_kernel(ao_ref, ho_ref, co_ref, ag_ref, hg_ref, cg_ref,
                  ga_ref, gb_ref, gc_ref, x_ref, wup_ref, wout_ref, ones_ref,
                  hgain_ref, cgain_ref, lng_ref, lnb_ref, y_ref):
    head_ones = ones_ref[...]
    for c in range(MERGE_ROWS // MERGE_CHAIN_ROWS):
        rows = slice(c * MERGE_CHAIN_ROWS, (c + 1) * MERGE_CHAIN_ROWS)
        y_a = ao_ref[rows, :] * _silu(ag_ref[rows, :].astype(F32))
        y_b = _head_rmsnorm(ho_ref[rows, :], head_ones, hgain_ref[...]) * _silu(
            hg_ref[rows, :].astype(F32))
        y_c = _head_rmsnorm(co_ref[rows, :], head_ones, cgain_ref[...]) * _silu(
            cg_ref[rows, :].astype(F32))
        merged = None
        for n, (y, g_ref) in enumerate(((y_a, ga_ref), (y_b, gb_ref), (y_c, gc_ref))):
            up = jnp.dot(y.astype(BF16), wup_ref[n], preferred_element_type=F32)
            term = _sigmoid(g_ref[rows, :]) * up.astype(BF16)
            merged = term if merged is None else merged + term
        out = jnp.dot(merged, wout_ref[...], preferred_element_type=F32)
        r = DEEPNORM_ALPHA * x_ref[rows, :] + out
        mu = jnp.mean(r, axis=-1, keepdims=True)
        d = r - mu
        var = jnp.mean(d * d, axis=-1, keepdims=True)
        y_ref[rows, :] = d * lax.rsqrt(var + NORM_EPS) * lng_ref[...] + lnb_ref[...]


MERGE_ROWS = 512
MERGE_CHAIN_ROWS = 256


def _merge(a_out, h_out, c_out, p, x2d, w_up, w_out, head_ones, hgain, cgain, ln_g, ln_b):
    n = x2d.shape[0]
    tn = MERGE_ROWS
    w = BRANCH_WIDTH
    row_w = lambda cb: pl.BlockSpec((tn, w), lambda i: (i, cb))
    row_d = lambda cb: pl.BlockSpec((tn, D_MODEL), lambda i: (i, cb))
    const = lambda shape: pl.BlockSpec(shape, lambda i: tuple(0 for _ in shape))
    gate0 = COL_GATES // D_MODEL
    return pl.pallas_call(
        _merge_kernel,
        grid=(n // tn,),
        in_specs=[
            row_w(0), row_w(0), row_w(0),
            row_w((COL_A + 3 * w) // w), row_w((COL_B + 3 * w) // w), row_w(COL_C_G // w),
            row_d(gate0), row_d(gate0 + 1), row_d(gate0 + 2),
            row_d(0),
            const((3, w, D_MODEL)), const((D_MODEL, D_MODEL)), const((w, w)),
            const((1, w)), const((1, w)), const((1, D_MODEL)), const((1, D_MODEL)),
        ],
        out_specs=pl.BlockSpec((tn, D_MODEL), lambda i: (i, 0)),
        out_shape=jax.ShapeDtypeStruct((n, D_MODEL), F32),
        compiler_params=pltpu.CompilerParams(
            dimension_semantics=("arbitrary",),
            vmem_limit_bytes=VMEM_LIMIT_BYTES),
        name="merge_out_norm",
    )(a_out, h_out, c_out, p, p, p, p, p, p, x2d, w_up, w_out, head_ones,
      hgain, cgain, ln_g, ln_b)


def _pack_w_in(w):
    bw = BRANCH_WIDTH
    a = w[:, 0:4 * bw]
    b = w[:, 4 * bw:8 * bw]
    c0 = 8 * bw
    c_q = w[:, c0:c0 + GLA_KEY_WIDTH]
    c_k = w[:, c0 + GLA_KEY_WIDTH:c0 + 2 * GLA_KEY_WIDTH]
    c_v = w[:, c0 + 2 * GLA_KEY_WIDTH:c0 + 2 * GLA_KEY_WIDTH + bw]
    c_g = w[:, c0 + 2 * GLA_KEY_WIDTH + bw:c0 + 2 * GLA_KEY_WIDTH + 2 * bw]
    r0 = c0 + 2 * GLA_KEY_WIDTH + 2 * bw
    c_r = w[:, r0:r0 + GLA_GATE_RANK]
    gates = w[:, r0 + GLA_GATE_RANK:]
    pad = jnp.zeros((w.shape[0], IN_PACKED - COL_C_R - GLA_GATE_RANK), w.dtype)
    packed = jnp.concatenate([a, b, gates, c_v, c_g, c_q, c_k, c_r, pad], axis=1)
    return packed.astype(BF16)


def kernel(x, w_in, gla_gate_w2, gla_gate_b, hgrn_lb_logits, hgrn_norm_g, gla_norm_g,
           w_up, w_out, ln_g, ln_b):
    batch, seq, d = x.shape
    n = batch * seq
    lb_soft = jax.nn.softmax(hgrn_lb_logits.astype(F32), axis=0)
    lower_bounds = jnp.cumsum(lb_soft, axis=0) - lb_soft[0:1]
    lvl, incl3 = _recurrence_constants()
    hd = np.arange(BRANCH_WIDTH) // HEAD_DIM
    head_ones = jnp.asarray((hd[:, None] == hd[None, :]).astype(np.float32), dtype=BF16)

    x2d = x.reshape(n, d)
    for layer in range(DEPTH):
        w_packed = _pack_w_in(w_in[layer])
        w2_pad = jnp.zeros((GLA_KEY_WIDTH, GLA_KEY_WIDTH), BF16).at[:GLA_GATE_RANK].set(
            gla_gate_w2[layer].astype(BF16))
        p, zf, r = _inproj(x2d, w_packed)
        p3 = p.reshape(batch, seq, IN_PACKED)
        a_out = _attention(p3)
        h_out = _hgrn(p3, zf.reshape(batch, seq, -1), lower_bounds[layer][None, :], lvl, incl3)
        c_out = _gla(p3, r.reshape(batch, seq, -1), w2_pad, gla_gate_b[layer][None, :],
                     lvl, incl3)
        x2d = _merge(a_out.reshape(n, -1), h_out.reshape(n, -1), c_out.reshape(n, -1),
                     p, x2d, w_up[layer].astype(BF16), w_out[layer].astype(BF16), head_ones,
                     hgrn_norm_g[layer][None, :], gla_norm_g[layer][None, :],
                     ln_g[layer][None, :], ln_b[layer][None, :])
    return x2d.reshape(batch, seq, d)
```

```python
import functools

import numpy as np
import jax
import jax.numpy as jnp
from jax import lax
from jax.experimental import pallas as pl
from jax.experimental.pallas import tpu as pltpu

F32 = jnp.float32
BF16 = jnp.bfloat16

D_MODEL = 1024
DEPTH = 2
BRANCH_WIDTH = D_MODEL // 4
N_HEADS = 4
HEAD_DIM = BRANCH_WIDTH // N_HEADS
GLA_KEY_DIM = HEAD_DIM // 2
GLA_KEY_WIDTH = N_HEADS * GLA_KEY_DIM
GLA_GATE_RANK = 16
GLA_TAU = 16.0
HGRN_EXP_CLIP = 60.0
NORM_EPS = 1e-5
DEEPNORM_ALPHA = (2 * DEPTH) ** 0.25

LANES = 128
VMEM_LIMIT_BYTES = 56 * 1024 * 1024

COL_A = 0
COL_B = 1024
COL_GATES = 2048
COL_C_V = 5120
COL_C_G = 5376
COL_C_Q = 5632
COL_C_K = 5760
COL_C_R = 5888
IN_PACKED = 6144

CHUNK = 64
N_LEVELS = 6
ATTN_BLOCK = 256
ATTN_STEPS_PER_ITER = 4
LOG2E = 1.4426950408889634


def _nt_dot(a, b):
    return lax.dot_general(a, b, (((1,), (1,)), ((), ())), preferred_element_type=F32)


def _tn_dot(a, b):
    return lax.dot_general(a, b, (((0,), (0,)), ((), ())), preferred_element_type=F32)


def _neg_softplus_neg_abs(z):
    return -jnp.log(1.0 + jnp.exp(-jnp.abs(z)))


def _log_sigmoid(z):
    return jnp.minimum(z, 0.0) + _neg_softplus_neg_abs(z)


def _sigmoid(z):
    return 1.0 / (1.0 + jnp.exp(-z))


INPROJ_BM = 2048
INPROJ_BN = 1024


def _inproj_kernel(x_ref, w_ref, p_ref, zf_ref, r_ref):
    j = pl.program_id(1)
    acc = jnp.dot(x_ref[...].astype(BF16), w_ref[...], preferred_element_type=F32)
    col = j * INPROJ_BN + lax.broadcasted_iota(jnp.int32, (1, INPROJ_BN), 1)
    scale = jnp.where(col < COL_A + BRANCH_WIDTH, HEAD_DIM ** -0.5, 1.0)
    p_ref[...] = (acc * scale).astype(BF16)

    @pl.when(j == COL_B // INPROJ_BN)
    def _():
        c0 = COL_B % INPROJ_BN
        zf_ref[...] = acc[:, c0:c0 + BRANCH_WIDTH]

    @pl.when(j == COL_C_R // INPROJ_BN)
    def _():
        c0 = COL_C_R % INPROJ_BN
        r_ref[...] = acc[:, c0:c0 + LANES]


def _inproj(x2d, w_packed):
    n = x2d.shape[0]
    bm, bn = INPROJ_BM, INPROJ_BN
    return pl.pallas_call(
        _inproj_kernel,
        grid=(n // bm, IN_PACKED // bn),
        in_specs=[
            pl.BlockSpec((bm, D_MODEL), lambda i, j: (i, 0)),
            pl.BlockSpec((D_MODEL, bn), lambda i, j: (0, j)),
        ],
        out_specs=[
            pl.BlockSpec((bm, bn), lambda i, j: (i, j)),
            pl.BlockSpec((bm, BRANCH_WIDTH), lambda i, j: (i, 0)),
            pl.BlockSpec((bm, LANES), lambda i, j: (i, 0)),
        ],
        out_shape=[
            jax.ShapeDtypeStruct((n, IN_PACKED), BF16),
            jax.ShapeDtypeStruct((n, BRANCH_WIDTH), F32),
            jax.ShapeDtypeStruct((n, LANES), F32),
        ],
        compiler_params=pltpu.CompilerParams(
            dimension_semantics=("arbitrary", "arbitrary"),
            vmem_limit_bytes=VMEM_LIMIT_BYTES),
        name="inproj",
    )(x2d, w_packed)


def _attn_kernel(q_ref, k_ref, v_ref, tri_ref, o_ref,
                 acc_ref, car_ref, z0_ref, z1_ref, w_ref):
    blk = q_ref.shape[1]
    qi = pl.program_id(1)
    lane = lax.broadcasted_iota(jnp.int32, (1, LANES), 1)
    q_heads = []
    for h in range(N_HEADS):
        q_tile = q_ref[0, :, (h // 2) * LANES:(h // 2 + 1) * LANES]
        keep = (lane < HEAD_DIM) if h % 2 == 0 else (lane >= HEAD_DIM)
        q_heads.append(jnp.where(keep, q_tile, jnp.zeros_like(q_tile)))
    row = lax.broadcasted_iota(jnp.int32, (blk, blk), 0)
    col = lax.broadcasted_iota(jnp.int32, (blk, blk), 1)
    causal = col < row

    acc_ref[...] = jnp.zeros_like(acc_ref)
    car_ref[...] = jnp.zeros_like(car_ref)

    def lanes_of(h):
        return slice((h // 2) * LANES, (h // 2 + 1) * LANES)

    def key_rows(j):
        return pl.ds(pl.multiple_of((qi - j) * blk, blk), blk)

    z_refs = (z0_ref, z1_ref)

    def scores_h(h, j, parity):
        k_tile = k_ref[0, key_rows(j), lanes_of(h)]
        z_refs[parity][h] = _nt_dot(q_heads[h], k_tile)

    def sums_h(h, parity, diagonal=False):
        z = z_refs[parity][h]
        softplus = jnp.maximum(z, 0.0) + jnp.log(
            1.0 + jnp.exp2(jnp.abs(z) * (-LOG2E)))
        if diagonal:
            softplus = jnp.where(causal, softplus, 0.0)
        car = car_ref[h]
        z_car = z + jnp.concatenate([car] * (blk // LANES), axis=1)
        suffix = jnp.dot(softplus.astype(BF16), tri_ref[...], preferred_element_type=F32)
        w = jnp.exp((z_car + suffix).astype(BF16))
        if diagonal:
            w = jnp.where(causal, w, jnp.zeros_like(w))
        w_ref[h] = w
        car_ref[h] = car + jnp.broadcast_to(suffix[:, 0:1], (blk, LANES))

    def values_h(h, j):
        v_tile = v_ref[0, key_rows(j), lanes_of(h)]
        acc_ref[h] += jnp.dot(w_ref[h], v_tile, preferred_element_type=F32)

    def step(t, parity):
        for h in range(N_HEADS):
            scores_h(h, t, parity)
            values_h(h, t - 2)
            sums_h(h, 1 - parity)

    def drain(parity):
        for h in range(N_HEADS):
            values_h(h, qi - 1)
            sums_h(h, parity)
            values_h(h, qi)

    for h in range(N_HEADS):
        scores_h(h, 0, 0)

    @pl.when(qi == 0)
    def _():
        for h in range(N_HEADS):
            sums_h(h, 0, True)
            values_h(h, 0)

    @pl.when(qi == 1)
    def _():
        for h in range(N_HEADS):
            scores_h(h, 1, 1)
            sums_h(h, 0, True)
        drain(1)

    @pl.when(qi >= 2)
    def _():
        for h in range(N_HEADS):
            scores_h(h, 1, 1)
            sums_h(h, 0, True)
        step(2, 0)

        n_steady = qi - 2
        pairs_per_iter = ATTN_STEPS_PER_ITER // 2

        def body(i, c):
            t = 3 + ATTN_STEPS_PER_ITER * i
            for n in range(pairs_per_iter):
                step(t + 2 * n, 1)
                step(t + 2 * n + 1, 0)
            return c

        lax.fori_loop(0, n_steady // ATTN_STEPS_PER_ITER, body, 0)
        t_rest = 3 + (n_steady // ATTN_STEPS_PER_ITER) * ATTN_STEPS_PER_ITER
        for n in range(pairs_per_iter - 1):
            @pl.when((n_steady % ATTN_STEPS_PER_ITER) // 2 > n)
            def _():
                step(t_rest + 2 * n, 1)
                step(t_rest + 2 * n + 1, 0)

        @pl.when(qi % 2 == 1)
        def _():
            step(qi, 1)
            drain(1)

        @pl.when(qi % 2 == 0)
        def _():
            drain(0)

    for lt in range(N_HEADS // 2):
        o_ref[0, :, lt * LANES:(lt + 1) * LANES] = jnp.where(
            lane < HEAD_DIM, acc_ref[2 * lt], acc_ref[2 * lt + 1])


def _attention(p3):
    batch, seq, _ = p3.shape
    blk = ATTN_BLOCK
    w = BRANCH_WIDTH
    qkv3 = p3
    j = np.arange(blk)
    tri = jnp.asarray(-(j[:, None] >= j[None, :]).astype(np.float32), dtype=BF16)
    return pl.pallas_call(
        _attn_kernel,
        grid=(batch, seq // blk),
        in_specs=[
            pl.BlockSpec((1, blk, w), lambda b, i: (b, i, 0)),
            pl.BlockSpec((1, seq, w), lambda b, i: (b, 0, 1)),
            pl.BlockSpec((1, seq, w), lambda b, i: (b, 0, 2)),
            pl.BlockSpec((blk, blk), lambda b, i: (0, 0)),
        ],
        out_specs=pl.BlockSpec((1, blk, w), lambda b, i: (b, i, 0)),
        out_shape=jax.ShapeDtypeStruct((batch, seq, w), F32),
        scratch_shapes=[
            pltpu.VMEM((N_HEADS, blk, LANES), F32),
            pltpu.VMEM((N_HEADS, blk, LANES), F32),
            pltpu.VMEM((N_HEADS, blk, blk), F32),
            pltpu.VMEM((N_HEADS, blk, blk), F32),
            pltpu.VMEM((N_HEADS, blk, blk), BF16),
        ],
        compiler_params=pltpu.CompilerParams(
            dimension_semantics=("arbitrary", "arbitrary"),
            vmem_limit_bytes=VMEM_LIMIT_BYTES),
        name="stickbreak_attn",
    )(qkv3, qkv3, qkv3, tri)


def _recurrence_constants():
    t = np.arange(CHUNK)
    x = t[:, None] ^ t[None, :]
    top_bit = np.floor(np.log2(np.maximum(x, 1))).astype(np.int32)
    lvl = np.where(t[:, None] > t[None, :], top_bit,
                   np.where(t[:, None] == t[None, :], N_LEVELS, -1)).astype(np.int32)
    lvl_all = np.tile(lvl, (1, N_HEADS))
    incl = (t[None, :] <= t[:, None]).astype(np.float32)
    incl3 = np.concatenate([incl, incl, incl], axis=1)
    return jnp.asarray(lvl_all), jnp.asarray(incl3, dtype=BF16)


def _run_chunks(q_of, k_ref, v_of, lf_ref, incl3_ref, lvl_ref, o_ref, st_ref, cum_ref,
                n_chunks, key_dim, pad):
    c = CHUNK
    incl3 = incl3_ref[...]
    lvl = lvl_ref[...]
    chunks = range(n_chunks)
    q = [q_of(ci * c) for ci in chunks]
    k = [k_ref[ci * c:(ci + 1) * c, :] for ci in chunks]
    v_bf = [v_of(ci * c) for ci in chunks]
    lf = [lf_ref[pad + ci * c:pad + (ci + 1) * c, :] for ci in chunks]
    wk = q[0].shape[1]
    wv = v_bf[0].shape[1]

    lane_k = lax.broadcasted_iota(jnp.int32, (1, wk), 1)
    lane_v = lax.broadcasted_iota(jnp.int32, (1, wv), 1)
    row = lax.broadcasted_iota(jnp.int32, (c, 1), 0)
    same_head = (lax.broadcasted_iota(jnp.int32, (wv, wk), 0) // HEAD_DIM
                 == lax.broadcasted_iota(jnp.int32, (wv, wk), 1) // key_dim)

    def block_diag_rows(a, lane, width):
        zero = jnp.zeros_like(a)
        return jnp.concatenate(
            [jnp.where((lane >= h * width) & (lane < (h + 1) * width), a, zero)
             for h in range(N_HEADS)], axis=0)

    cum = []
    for ci in chunks:
        h1 = lf[ci].astype(BF16)
        r1 = lf[ci] - h1.astype(F32)
        h2 = r1.astype(BF16)
        h3 = (r1 - h2.astype(F32)).astype(BF16)
        cum.append(jnp.dot(incl3, jnp.concatenate([h1, h2, h3], axis=0),
                           preferred_element_type=F32))
        cum_ref[ci] = cum[ci]
    last = [cum[ci][c - 1:c, :] for ci in chunks]

    upd = [jnp.dot(v_bf[ci].astype(F32).T.astype(BF16),
                   (k[ci] * jnp.exp(last[ci] - cum[ci])).astype(BF16),
                   preferred_element_type=F32)
           for ci in chunks]

    scores = [jnp.zeros((c, N_HEADS * c), F32) for _ in chunks]
    for level in range(N_LEVELS + 1):
        b = 1 << level
        for ci in chunks:
            if level == N_LEVELS:
                ql, kl = q[ci], k[ci]
            else:
                if level == 0:
                    arg = jnp.where((row & 1) == 1, lf[ci], 0.0)
                elif level == 1:
                    mids = [jnp.where((row[:8] & 4) == 0,
                                      jnp.broadcast_to(cum_ref[ci, m + 1:m + 2, :], (8, wk)),
                                      jnp.broadcast_to(cum_ref[ci, m + 5:m + 6, :], (8, wk)))
                            for m in range(0, c, 8)]
                    arg = -jnp.abs(cum[ci] - jnp.concatenate(mids, axis=0))
                else:
                    mids = [jnp.broadcast_to(cum_ref[ci, m:m + 1, :], (2 * b, wk))
                            for m in range(b - 1, c, 2 * b)]
                    mid = mids[0] if len(mids) == 1 else jnp.concatenate(mids, axis=0)
                    arg = -jnp.abs(cum[ci] - mid)
                e = jnp.exp(arg)
                ql, kl = q[ci] * e, k[ci] * e
            s_l = _nt_dot(ql.astype(BF16),
                          block_diag_rows(kl.astype(BF16), lane_k, key_dim))
            scores[ci] = jnp.where(lvl == level, s_l, scores[ci])

    st = st_ref[...]
    o = []
    for ci in chunks:
        st_bf = st.astype(BF16)
        o.append(_nt_dot((q[ci] * jnp.exp(cum[ci])).astype(BF16),
                         jnp.where(same_head, st_bf, jnp.zeros_like(st_bf))))
        st = jnp.exp(last[ci]) * st + upd[ci]
    st_ref[...] = st

    for ci in chunks:
        o_ref[0, ci * c:(ci + 1) * c, :] = o[ci] + jnp.dot(
            scores[ci].astype(BF16), block_diag_rows(v_bf[ci], lane_v, HEAD_DIM),
            preferred_element_type=F32)


_LF_PAD = 8


def _hgrn_kernel(zf_ref, vi_ref, q_ref, lb_ref, incl3_ref, lvl_ref, o_ref,
                 st_ref, cum_ref, lf_ref, k_ref):
    tt = zf_ref.shape[1]

    @pl.when(pl.program_id(1) == 0)
    def _():
        st_ref[...] = jnp.zeros_like(st_ref)
        lf_ref[...] = jnp.zeros_like(lf_ref)

    zf = zf_ref[0]
    lb = lb_ref[...]
    lf_ref[_LF_PAD:_LF_PAD + tt, :] = _log_sigmoid(zf) + jnp.log(
        1.0 + lb * jnp.exp(jnp.minimum(-zf, HGRN_EXP_CLIP)))
    k_ref[...] = (1.0 - lb) * _sigmoid(-zf)

    _run_chunks(lambda r0: q_ref[0, r0:r0 + CHUNK, :].astype(F32), k_ref,
                lambda r0: vi_ref[0, r0:r0 + CHUNK, :], lf_ref, incl3_ref, lvl_ref,
                o_ref, st_ref, cum_ref, tt // CHUNK, HEAD_DIM, _LF_PAD)


def _gla_kernel(q_ref, kin_ref, v_ref, r_ref, w2_ref, b_ref, incl3_ref, lvl_ref, o_ref,
                st_ref, cum_ref, lf_ref, k_ref, qs_ref):
    tt = q_ref.shape[1]

    @pl.when(pl.program_id(1) == 0)
    def _():
        st_ref[...] = jnp.zeros_like(st_ref)
        lf_ref[...] = jnp.zeros_like(lf_ref)

    gate_pre = jnp.dot(r_ref[0].astype(BF16), w2_ref[...],
                       preferred_element_type=F32) + b_ref[...]
    lf_ref[_LF_PAD:_LF_PAD + tt, :] = _log_sigmoid(gate_pre) / GLA_TAU
    k_ref[...] = kin_ref[0].astype(F32)
    qs_ref[...] = q_ref[0].astype(F32) * (GLA_KEY_DIM ** -0.5)

    _run_chunks(lambda r0: qs_ref[r0:r0 + CHUNK, :], k_ref,
                lambda r0: v_ref[0, r0:r0 + CHUNK, :], lf_ref, incl3_ref, lvl_ref,
                o_ref, st_ref, cum_ref, tt // CHUNK, GLA_KEY_DIM, _LF_PAD)


_REC_BLOCK = 512


def _const_spec(shape):
    return pl.BlockSpec(shape, lambda b, t: tuple(0 for _ in shape))


def _hgrn(p3, zf3, lb_row, lvl, incl3):
    batch, seq, _ = p3.shape
    tt = _REC_BLOCK
    w = BRANCH_WIDTH
    cb = COL_B // w
    return pl.pallas_call(
        _hgrn_kernel,
        grid=(batch, seq // tt),
        in_specs=[
            pl.BlockSpec((1, tt, w), lambda b, t: (b, t, 0)),
            pl.BlockSpec((1, tt, w), lambda b, t: (b, t, cb + 1)),
            pl.BlockSpec((1, tt, w), lambda b, t: (b, t, cb + 2)),
            _const_spec((1, w)),
            _const_spec(incl3.shape),
            _const_spec(lvl.shape),
        ],
        out_specs=pl.BlockSpec((1, tt, w), lambda b, t: (b, t, 0)),
        out_shape=jax.ShapeDtypeStruct((batch, seq, w), F32),
        scratch_shapes=[
            pltpu.VMEM((w, w), F32),
            pltpu.VMEM((tt // CHUNK, CHUNK, w), F32),
            pltpu.VMEM((tt + 2 * _LF_PAD, w), F32),
            pltpu.VMEM((tt, w), F32),
        ],
        compiler_params=pltpu.CompilerParams(
            dimension_semantics=("arbitrary", "arbitrary"),
            vmem_limit_bytes=VMEM_LIMIT_BYTES),
        name="hgrn2_recurrence",
    )(zf3, p3, p3, lb_row, incl3, lvl)


def _gla(p3, r3, w2_pad, gate_b, lvl, incl3):
    batch, seq, _ = p3.shape
    tt = _REC_BLOCK
    w = BRANCH_WIDTH
    kw = GLA_KEY_WIDTH
    return pl.pallas_call(
        _gla_kernel,
        grid=(batch, seq // tt),
        in_specs=[
            pl.BlockSpec((1, tt, kw), lambda b, t: (b, t, COL_C_Q // kw)),
            pl.BlockSpec((1, tt, kw), lambda b, t: (b, t, COL_C_K // kw)),
            pl.BlockSpec((1, tt, w), lambda b, t: (b, t, COL_C_V // w)),
            pl.BlockSpec((1, tt, kw), lambda b, t: (b, t, 0)),
            _const_spec((kw, kw)),
            _const_spec((1, kw)),
            _const_spec(incl3.shape),
            _const_spec(lvl.shape),
        ],
        out_specs=pl.BlockSpec((1, tt, w), lambda b, t: (b, t, 0)),
        out_shape=jax.ShapeDtypeStruct((batch, seq, w), F32),
        scratch_shapes=[
            pltpu.VMEM((w, kw), F32),
            pltpu.VMEM((tt // CHUNK, CHUNK, kw), F32),
            pltpu.VMEM((tt + 2 * _LF_PAD, kw), F32),
            pltpu.VMEM((tt, kw), F32),
            pltpu.VMEM((tt, kw), F32),
        ],
        compiler_params=pltpu.CompilerParams(
            dimension_semantics=("arbitrary", "arbitrary"),
            vmem_limit_bytes=VMEM_LIMIT_BYTES),
        name="gla_recurrence",
    )(p3, p3, p3, r3, w2_pad, gate_b, incl3, lvl)


def _silu(g):
    return g * _sigmoid(g)


def _head_rmsnorm(o, head_ones, gain):
    sq = o * o
    hi = sq.astype(BF16)
    lo = (sq - hi.astype(F32)).astype(BF16)
    ms = (jnp.dot(hi, head_ones, preferred_element_type=F32)
          + jnp.dot(lo, head_ones, preferred_element_type=F32)) * (1.0 / HEAD_DIM)
    return o * lax.rsqrt(ms + NORM_EPS) * gain


def _merge_kernel(ao_ref, ho_ref, co_ref, ag_ref, hg_ref, cg_ref,
                  ga_ref, gb_ref, gc_ref, x_ref, wup_ref, wout_ref, ones_ref,
                  hgain_ref, cgain_ref, lng_ref, lnb_ref, y_ref):
    head_ones = ones_ref[...]
    chains = [slice(c * MERGE_CHAIN_ROWS, (c + 1) * MERGE_CHAIN_ROWS)
              for c in range(MERGE_ROWS // MERGE_CHAIN_ROWS)]
    gate_refs = (ga_ref, gb_ref, gc_ref)
    ys = []
    for rows in chains:
        y_a = ao_ref[rows, :] * _silu(ag_ref[rows, :].astype(F32))
        y_b = _head_rmsnorm(ho_ref[rows, :], head_ones, hgain_ref[...]) * _silu(
            hg_ref[rows, :].astype(F32))
        y_c = _head_rmsnorm(co_ref[rows, :], head_ones, cgain_ref[...]) * _silu(
            cg_ref[rows, :].astype(F32))
        ys.append((y_a.astype(BF16), y_b.astype(BF16), y_c.astype(BF16)))
    merged = [None] * len(chains)
    for n in range(3):
        for ci, rows in enumerate(chains):
            up = jnp.dot(ys[ci][n], wup_ref[n], preferred_element_type=F32)
            term = _sigmoid(gate_refs[n][rows, :]) * up.astype(BF16)
            merged[ci] = term if n == 0 else merged[ci] + term
    outs = [jnp.dot(m, wout_ref[...], preferred_element_type=F32) for m in merged]
    for rows, out in zip(chains, outs):
        r = DEEPNORM_ALPHA * x_ref[rows, :] + out
        mu = jnp.mean(r, axis=-1, keepdims=True)
        d = r - mu
        var = jnp.mean(d * d, axis=-1, keepdims=True)
        y_ref[rows, :] = d * lax.rsqrt(var + NORM_EPS) * lng_ref[...] + lnb_ref[...]


MERGE_ROWS = 512
MERGE_CHAIN_ROWS = 256


def _merge(a_out, h_out, c_out, p, x2d, w_up, w_out, head_ones, hgain, cgain, ln_g, ln_b):
    n = x2d.shape[0]
    tn = MERGE_ROWS
    w = BRANCH_WIDTH
    row_w = lambda cb: pl.BlockSpec((tn, w), lambda i: (i, cb))
    row_d = lambda cb: pl.BlockSpec((tn, D_MODEL), lambda i: (i, cb))
    const = lambda shape: pl.BlockSpec(shape, lambda i: tuple(0 for _ in shape))
    gate0 = COL_GATES // D_MODEL
    return pl.pallas_call(
        _merge_kernel,
        grid=(n // tn,),
        in_specs=[
            row_w(0), row_w(0), row_w(0),
            row_w((COL_A + 3 * w) // w), row_w((COL_B + 3 * w) // w), row_w(COL_C_G // w),
            row_d(gate0), row_d(gate0 + 1), row_d(gate0 + 2),
            row_d(0),
            const((3, w, D_MODEL)), const((D_MODEL, D_MODEL)), const((w, w)),
            const((1, w)), const((1, w)), const((1, D_MODEL)), const((1, D_MODEL)),
        ],
        out_specs=pl.BlockSpec((tn, D_MODEL), lambda i: (i, 0)),
        out_shape=jax.ShapeDtypeStruct((n, D_MODEL), F32),
        compiler_params=pltpu.CompilerParams(
            dimension_semantics=("arbitrary",),
            vmem_limit_bytes=VMEM_LIMIT_BYTES),
        name="merge_out_norm",
    )(a_out, h_out, c_out, p, p, p, p, p, p, x2d, w_up, w_out, head_ones,
      hgain, cgain, ln_g, ln_b)


def _pack_w_in(w):
    bw = BRANCH_WIDTH
    a = w[:, 0:4 * bw]
    b = w[:, 4 * bw:8 * bw]
    c0 = 8 * bw
    c_q = w[:, c0:c0 + GLA_KEY_WIDTH]
    c_k = w[:, c0 + GLA_KEY_WIDTH:c0 + 2 * GLA_KEY_WIDTH]
    c_v = w[:, c0 + 2 * GLA_KEY_WIDTH:c0 + 2 * GLA_KEY_WIDTH + bw]
    c_g = w[:, c0 + 2 * GLA_KEY_WIDTH + bw:c0 + 2 * GLA_KEY_WIDTH + 2 * bw]
    r0 = c0 + 2 * GLA_KEY_WIDTH + 2 * bw
    c_r = w[:, r0:r0 + GLA_GATE_RANK]
    gates = w[:, r0 + GLA_GATE_RANK:]
    pad = jnp.zeros((w.shape[0], IN_PACKED - COL_C_R - GLA_GATE_RANK), w.dtype)
    packed = jnp.concatenate([a, b, gates, c_v, c_g, c_q, c_k, c_r, pad], axis=1)
    return packed.astype(BF16)


def kernel(x, w_in, gla_gate_w2, gla_gate_b, hgrn_lb_logits, hgrn_norm_g, gla_norm_g,
           w_up, w_out, ln_g, ln_b):
    batch, seq, d = x.shape
    n = batch * seq
    lb_soft = jax.nn.softmax(hgrn_lb_logits.astype(F32), axis=0)
    lower_bounds = jnp.cumsum(lb_soft, axis=0) - lb_soft[0:1]
    lvl, incl3 = _recurrence_constants()
    hd = np.arange(BRANCH_WIDTH) // HEAD_DIM
    head_ones = jnp.asarray((hd[:, None] == hd[None, :]).astype(np.float32), dtype=BF16)

    x2d = x.reshape(n, d)
    for layer in range(DEPTH):
        w_packed = _pack_w_in(w_in[layer])
        w2_pad = jnp.zeros((GLA_KEY_WIDTH, GLA_KEY_WIDTH), BF16).at[:GLA_GATE_RANK].set(
            gla_gate_w2[layer].astype(BF16))
        p, zf, r = _inproj(x2d, w_packed)
        p3 = p.reshape(batch, seq, IN_PACKED)
        a_out = _attention(p3)
        h_out = _hgrn(p3, zf.reshape(batch, seq, -1), lower_bounds[layer][None, :], lvl, incl3)
        c_out = _gla(p3, r.reshape(batch, seq, -1), w2_pad, gla_gate_b[layer][None, :],
                     lvl, incl3)
        x2d = _merge(a_out.reshape(n, -1), h_out.reshape(n, -1), c_out.reshape(n, -1),
                     p, x2d, w_up[layer].astype(BF16), w_out[layer].astype(BF16), head_ones,
                     hgrn_norm_g[layer][None, :], gla_norm_g[layer][None, :],
                     ln_g[layer][None, :], ln_b[layer][None, :])
    return x2d.reshape(batch, seq, d)
```

```python
import itertools

import numpy as np
import jax
import jax.numpy as jnp
from jax import lax
from jax.experimental import pallas as pl
from jax.experimental.pallas import tpu as pltpu

F32 = jnp.float32
BF16 = jnp.bfloat16

D_MODEL = 1024
DEPTH = 2
BRANCH_WIDTH = D_MODEL // 4
N_HEADS = 4
HEAD_DIM = BRANCH_WIDTH // N_HEADS
GLA_KEY_DIM = HEAD_DIM // 2
GLA_KEY_WIDTH = N_HEADS * GLA_KEY_DIM
GLA_GATE_RANK = 16
GLA_TAU = 16.0
HGRN_EXP_CLIP = 60.0
NORM_EPS = 1e-5
DEEPNORM_ALPHA = (2 * DEPTH) ** 0.25

LANES = 128
VMEM_LIMIT_BYTES = 56 * 1024 * 1024

COL_A = 0
COL_B = 1024
COL_GATES = 2048
COL_C_V = 5120
COL_C_G = 5376
COL_C_Q = 5632
COL_C_K = 5760
COL_C_R = 5888
IN_PACKED = 6144

CHUNK = 64
N_LEVELS = 6
ATTN_BLOCK = 256
ATTN_STEPS_PER_ITER = 4
LOG2E = 1.4426950408889634


def _nt_dot(a, b):
    return lax.dot_general(a, b, (((1,), (1,)), ((), ())), preferred_element_type=F32)


def _tn_dot(a, b):
    return lax.dot_general(a, b, (((0,), (0,)), ((), ())), preferred_element_type=F32)


def _neg_softplus_neg_abs(z):
    return -jnp.log(1.0 + jnp.exp(-jnp.abs(z)))


def _log_sigmoid(z):
    return jnp.minimum(z, 0.0) + _neg_softplus_neg_abs(z)


def _sigmoid(z):
    return 1.0 / (1.0 + jnp.exp(-z))


INPROJ_BM = 2048
INPROJ_BN = 1024


def _inproj_kernel(x_ref, w_ref, p_ref, zf_ref, r_ref):
    j = pl.program_id(1)
    acc = jnp.dot(x_ref[...].astype(BF16), w_ref[...], preferred_element_type=F32)
    col = j * INPROJ_BN + lax.broadcasted_iota(jnp.int32, (1, INPROJ_BN), 1)
    scale = jnp.where(col < COL_A + BRANCH_WIDTH, HEAD_DIM ** -0.5, 1.0)
    p_ref[...] = (acc * scale).astype(BF16)

    @pl.when(j == COL_B // INPROJ_BN)
    def _():
        c0 = COL_B % INPROJ_BN
        zf_ref[...] = acc[:, c0:c0 + BRANCH_WIDTH]

    @pl.when(j == COL_C_R // INPROJ_BN)
    def _():
        c0 = COL_C_R % INPROJ_BN
        r_ref[...] = acc[:, c0:c0 + LANES]


def _inproj(x2d, w_packed, layer):
    n = x2d.shape[0]
    bm, bn = INPROJ_BM, INPROJ_BN
    return pl.pallas_call(
        _inproj_kernel,
        grid=(n // bm, IN_PACKED // bn),
        in_specs=[
            pl.BlockSpec((bm, D_MODEL), lambda i, j: (i, 0)),
            pl.BlockSpec((None, D_MODEL, bn), lambda i, j: (layer, 0, j)),
        ],
        out_specs=[
            pl.BlockSpec((bm, bn), lambda i, j: (i, j)),
            pl.BlockSpec((bm, BRANCH_WIDTH), lambda i, j: (i, 0)),
            pl.BlockSpec((bm, LANES), lambda i, j: (i, 0)),
        ],
        out_shape=[
            jax.ShapeDtypeStruct((n, IN_PACKED), BF16),
            jax.ShapeDtypeStruct((n, BRANCH_WIDTH), F32),
            jax.ShapeDtypeStruct((n, LANES), F32),
        ],
        compiler_params=pltpu.CompilerParams(
            dimension_semantics=("arbitrary", "arbitrary"),
            vmem_limit_bytes=VMEM_LIMIT_BYTES),
        name="inproj",
    )(x2d, w_packed)


def _attn_kernel(q_ref, k_ref, v_ref, tri_ref, o_ref,
                 acc_ref, car_ref, z0_ref, z1_ref, w_ref):
    blk = q_ref.shape[1]
    qi = pl.program_id(1)
    lane = lax.broadcasted_iota(jnp.int32, (1, LANES), 1)
    q_heads = []
    for h in range(N_HEADS):
        q_tile = q_ref[0, :, (h // 2) * LANES:(h // 2 + 1) * LANES]
        keep = (lane < HEAD_DIM) if h % 2 == 0 else (lane >= HEAD_DIM)
        q_heads.append(jnp.where(keep, q_tile, jnp.zeros_like(q_tile)))
    row = lax.broadcasted_iota(jnp.int32, (blk, blk), 0)
    col = lax.broadcasted_iota(jnp.int32, (blk, blk), 1)
    causal = col < row

    acc_ref[...] = jnp.zeros_like(acc_ref)
    car_ref[...] = jnp.zeros_like(car_ref)

    def lanes_of(h):
        return slice((h // 2) * LANES, (h // 2 + 1) * LANES)

    def key_rows(j):
        return pl.ds(pl.multiple_of((qi - j) * blk, blk), blk)

    z_refs = (z0_ref, z1_ref)

    def scores_h(h, j, parity):
        k_tile = k_ref[0, key_rows(j), lanes_of(h)]
        z_refs[parity][h] = _nt_dot(q_heads[h], k_tile)

    def sums_h(h, parity, diagonal=False):
        z = z_refs[parity][h]
        softplus = jnp.maximum(z, 0.0) + jnp.log(
            1.0 + jnp.exp2(jnp.abs(z) * (-LOG2E)))
        if diagonal:
            softplus = jnp.where(causal, softplus, 0.0)
        car = car_ref[h]
        z_car = z + jnp.concatenate([car] * (blk // LANES), axis=1)
        suffix = jnp.dot(softplus.astype(BF16), tri_ref[...], preferred_element_type=F32)
        w = jnp.exp(z_car + suffix)
        if diagonal:
            w = jnp.where(causal, w, 0.0)
        w_ref[h] = w.astype(BF16)
        car_ref[h] = car + jnp.broadcast_to(suffix[:, 0:1], (blk, LANES))

    def values_h(h, j):
        v_tile = v_ref[0, key_rows(j), lanes_of(h)]
        acc_ref[h] += jnp.dot(w_ref[h], v_tile, preferred_element_type=F32)

    def step(t, parity):
        for h in range(N_HEADS):
            scores_h(h, t, parity)
            values_h(h, t - 2)
            sums_h(h, 1 - parity)

    def drain(parity):
        for h in range(N_HEADS):
            values_h(h, qi - 1)
            sums_h(h, parity)
            values_h(h, qi)

    for h in range(N_HEADS):
        scores_h(h, 0, 0)

    @pl.when(qi == 0)
    def _():
        for h in range(N_HEADS):
            sums_h(h, 0, True)
            values_h(h, 0)

    @pl.when(qi == 1)
    def _():
        for h in range(N_HEADS):
            scores_h(h, 1, 1)
            sums_h(h, 0, True)
        drain(1)

    @pl.when(qi >= 2)
    def _():
        for h in range(N_HEADS):
            scores_h(h, 1, 1)
            sums_h(h, 0, True)
        step(2, 0)

        n_steady = qi - 2
        pairs_per_iter = ATTN_STEPS_PER_ITER // 2

        def body(i, c):
            t = 3 + ATTN_STEPS_PER_ITER * i
            for n in range(pairs_per_iter):
                step(t + 2 * n, 1)
                step(t + 2 * n + 1, 0)
            return c

        lax.fori_loop(0, n_steady // ATTN_STEPS_PER_ITER, body, 0)
        t_rest = 3 + (n_steady // ATTN_STEPS_PER_ITER) * ATTN_STEPS_PER_ITER
        for n in range(pairs_per_iter - 1):
            @pl.when((n_steady % ATTN_STEPS_PER_ITER) // 2 > n)
            def _():
                step(t_rest + 2 * n, 1)
                step(t_rest + 2 * n + 1, 0)

        @pl.when(qi % 2 == 1)
        def _():
            step(qi, 1)
            drain(1)

        @pl.when(qi % 2 == 0)
        def _():
            drain(0)

    for lt in range(N_HEADS // 2):
        o_ref[0, :, lt * LANES:(lt + 1) * LANES] = jnp.where(
            lane < HEAD_DIM, acc_ref[2 * lt], acc_ref[2 * lt + 1])


def _attention(p3):
    batch, seq, _ = p3.shape
    blk = ATTN_BLOCK
    w = BRANCH_WIDTH
    qkv3 = p3
    j = np.arange(blk)
    tri = jnp.asarray(-(j[:, None] >= j[None, :]).astype(np.float32), dtype=BF16)
    return pl.pallas_call(
        _attn_kernel,
        grid=(batch, seq // blk),
        in_specs=[
            pl.BlockSpec((1, blk, w), lambda b, i: (b, i, 0)),
            pl.BlockSpec((1, seq, w), lambda b, i: (b, 0, 1)),
            pl.BlockSpec((1, seq, w), lambda b, i: (b, 0, 2)),
            pl.BlockSpec((blk, blk), lambda b, i: (0, 0)),
        ],
        out_specs=pl.BlockSpec((1, blk, w), lambda b, i: (b, i, 0)),
        out_shape=jax.ShapeDtypeStruct((batch, seq, w), F32),
        scratch_shapes=[
            pltpu.VMEM((N_HEADS, blk, LANES), F32),
            pltpu.VMEM((N_HEADS, blk, LANES), F32),
            pltpu.VMEM((N_HEADS, blk, blk), F32),
            pltpu.VMEM((N_HEADS, blk, blk), F32),
            pltpu.VMEM((N_HEADS, blk, blk), BF16),
        ],
        compiler_params=pltpu.CompilerParams(
            dimension_semantics=("arbitrary", "arbitrary"),
            vmem_limit_bytes=VMEM_LIMIT_BYTES),
        name="stickbreak_attn",
    )(qkv3, qkv3, qkv3, tri)


def _recurrence_constants():
    t = np.arange(CHUNK)
    x = t[:, None] ^ t[None, :]
    top_bit = np.floor(np.log2(np.maximum(x, 1))).astype(np.int32)
    lvl = np.where(t[:, None] > t[None, :], top_bit,
                   np.where(t[:, None] == t[None, :], N_LEVELS, -1)).astype(np.int32)
    lvl_all = np.tile(lvl, (1, N_HEADS))
    incl = (t[None, :] <= t[:, None]).astype(np.float32)
    incl3 = np.concatenate([incl, incl, incl], axis=1)
    return jnp.asarray(lvl_all), jnp.asarray(incl3, dtype=BF16)


def _run_chunks(q_of, k_ref, v_of, lf_ref, incl3_ref, lvl_ref, o_ref, st_ref, cum_ref,
                n_chunks, key_dim):
    c = CHUNK
    incl3 = incl3_ref[...]
    lvl = lvl_ref[...]
    chunks = range(n_chunks)
    q = [q_of(ci * c) for ci in chunks]
    k = [k_ref[ci * c:(ci + 1) * c, :] for ci in chunks]
    v_bf = [v_of(ci * c) for ci in chunks]
    lf = [lf_ref[ci * c:(ci + 1) * c, :] for ci in chunks]
    wk = q[0].shape[1]
    wv = v_bf[0].shape[1]

    lane_k = lax.broadcasted_iota(jnp.int32, (1, wk), 1)
    lane_v = lax.broadcasted_iota(jnp.int32, (1, wv), 1)
    row = lax.broadcasted_iota(jnp.int32, (c, 1), 0)
    same_head = (lax.broadcasted_iota(jnp.int32, (wv, wk), 0) // HEAD_DIM
                 == lax.broadcasted_iota(jnp.int32, (wv, wk), 1) // key_dim)

    def block_diag_rows(a, lane, width):
        zero = jnp.zeros_like(a)
        return jnp.concatenate(
            [jnp.where((lane >= h * width) & (lane < (h + 1) * width), a, zero)
             for h in range(N_HEADS)], axis=0)

    cum = []
    for ci in chunks:
        h1 = lf[ci].astype(BF16)
        r1 = lf[ci] - h1.astype(F32)
        h2 = r1.astype(BF16)
        h3 = (r1 - h2.astype(F32)).astype(BF16)
        cum.append(jnp.dot(incl3, jnp.concatenate([h1, h2, h3], axis=0),
                           preferred_element_type=F32))
        cum_ref[ci] = cum[ci]
    last = [cum[ci][c - 1:c, :] for ci in chunks]
    yield

    upd = [jnp.dot(v_bf[ci].astype(F32).T.astype(BF16),
                   (k[ci] * jnp.exp(last[ci] - cum[ci])).astype(BF16),
                   preferred_element_type=F32)
           for ci in chunks]
    yield

    scores = [jnp.zeros((c, N_HEADS * c), F32) for _ in chunks]
    for level in range(N_LEVELS + 1):
        b = 1 << level
        for ci in chunks:
            if level == N_LEVELS:
                ql, kl = q[ci], k[ci]
            else:
                if level == 0:
                    arg = jnp.where((row & 1) == 1, lf[ci], 0.0)
                elif level == 1:
                    mids = [jnp.where((row[:8] & 4) == 0,
                                      jnp.broadcast_to(cum_ref[ci, m + 1:m + 2, :], (8, wk)),
                                      jnp.broadcast_to(cum_ref[ci, m + 5:m + 6, :], (8, wk)))
                            for m in range(0, c, 8)]
                    arg = -jnp.abs(cum[ci] - jnp.concatenate(mids, axis=0))
                else:
                    mids = [jnp.broadcast_to(cum_ref[ci, m:m + 1, :], (2 * b, wk))
                            for m in range(b - 1, c, 2 * b)]
                    mid = mids[0] if len(mids) == 1 else jnp.concatenate(mids, axis=0)
                    arg = -jnp.abs(cum[ci] - mid)
                e = jnp.exp(arg)
                ql, kl = q[ci] * e, k[ci] * e
            s_l = _nt_dot(ql.astype(BF16),
                          block_diag_rows(kl.astype(BF16), lane_k, key_dim))
            scores[ci] = jnp.where(lvl == level, s_l, scores[ci])
        yield

    st = st_ref[...]
    o = []
    for ci in chunks:
        st_bf = st.astype(BF16)
        o.append(_nt_dot((q[ci] * jnp.exp(cum[ci])).astype(BF16),
                         jnp.where(same_head, st_bf, jnp.zeros_like(st_bf))))
        st = jnp.exp(last[ci]) * st + upd[ci]
    st_ref[...] = st
    yield

    for ci in chunks:
        o_ref[0, ci * c:(ci + 1) * c, :] = o[ci] + jnp.dot(
            scores[ci].astype(BF16), block_diag_rows(v_bf[ci], lane_v, HEAD_DIM),
            preferred_element_type=F32)


def _recurrence_kernel(zf_ref, hv_ref, hq_ref, lb_ref,
                       cq_ref, ck_ref, cv_ref, r_ref, w2_ref, b_ref,
                       incl3_ref, lvl_ref, ho_ref, co_ref,
                       hst_ref, hcum_ref, hlf_ref, hk_ref,
                       cst_ref, ccum_ref, clf_ref, ckf_ref, cqs_ref):
    tt = zf_ref.shape[1]

    @pl.when(pl.program_id(1) == 0)
    def _():
        hst_ref[...] = jnp.zeros_like(hst_ref)
        cst_ref[...] = jnp.zeros_like(cst_ref)

    zf = zf_ref[0]
    lb = lb_ref[...]
    hlf_ref[...] = _log_sigmoid(zf) + jnp.log(
        1.0 + lb * jnp.exp(jnp.minimum(-zf, HGRN_EXP_CLIP)))
    hk_ref[...] = (1.0 - lb) * _sigmoid(-zf)

    gate_pre = jnp.dot(r_ref[0].astype(BF16), w2_ref[...],
                       preferred_element_type=F32) + b_ref[...]
    clf_ref[...] = _log_sigmoid(gate_pre) / GLA_TAU
    ckf_ref[...] = ck_ref[0].astype(F32)
    cqs_ref[...] = cq_ref[0].astype(F32) * (GLA_KEY_DIM ** -0.5)

    hgrn = _run_chunks(lambda r0: hq_ref[0, r0:r0 + CHUNK, :].astype(F32), hk_ref,
                       lambda r0: hv_ref[0, r0:r0 + CHUNK, :], hlf_ref, incl3_ref, lvl_ref,
                       ho_ref, hst_ref, hcum_ref, tt // CHUNK, HEAD_DIM)
    gla = _run_chunks(lambda r0: cqs_ref[r0:r0 + CHUNK, :], ckf_ref,
                      lambda r0: cv_ref[0, r0:r0 + CHUNK, :], clf_ref, incl3_ref, lvl_ref,
                      co_ref, cst_ref, ccum_ref, tt // CHUNK, GLA_KEY_DIM)
    for _ in itertools.zip_longest(hgrn, gla):
        pass


_REC_BLOCK = 512


def _const_spec(shape):
    return pl.BlockSpec(shape, lambda b, t: tuple(0 for _ in shape))


def _recurrences(p3, zf3, r3, lb_row, w2_pad, gate_b, lvl, incl3, layer):
    batch, seq, _ = p3.shape
    tt = _REC_BLOCK
    w = BRANCH_WIDTH
    kw = GLA_KEY_WIDTH
    cb = COL_B // w
    layer_spec = lambda shape: pl.BlockSpec(
        (None,) + shape, lambda b, t: (layer,) + tuple(0 for _ in shape))
    out_spec = pl.BlockSpec((1, tt, w), lambda b, t: (b, t, 0))
    out_shape = jax.ShapeDtypeStruct((batch, seq, w), F32)
    return pl.pallas_call(
        _recurrence_kernel,
        grid=(batch, seq // tt),
        in_specs=[
            pl.BlockSpec((1, tt, w), lambda b, t: (b, t, 0)),
            pl.BlockSpec((1, tt, w), lambda b, t: (b, t, cb + 1)),
            pl.BlockSpec((1, tt, w), lambda b, t: (b, t, cb + 2)),
            layer_spec((1, w)),
            pl.BlockSpec((1, tt, kw), lambda b, t: (b, t, COL_C_Q // kw)),
            pl.BlockSpec((1, tt, kw), lambda b, t: (b, t, COL_C_K // kw)),
            pl.BlockSpec((1, tt, w), lambda b, t: (b, t, COL_C_V // w)),
            pl.BlockSpec((1, tt, kw), lambda b, t: (b, t, 0)),
            layer_spec((kw, kw)),
            layer_spec((1, kw)),
            _const_spec(incl3.shape),
            _const_spec(lvl.shape),
        ],
        out_specs=[out_spec, out_spec],
        out_shape=[out_shape, out_shape],
        scratch_shapes=[
            pltpu.VMEM((w, w), F32),
            pltpu.VMEM((tt // CHUNK, CHUNK, w), F32),
            pltpu.VMEM((tt, w), F32),
            pltpu.VMEM((tt, w), F32),
            pltpu.VMEM((w, kw), F32),
            pltpu.VMEM((tt // CHUNK, CHUNK, kw), F32),
            pltpu.VMEM((tt, kw), F32),
            pltpu.VMEM((tt, kw), F32),
            pltpu.VMEM((tt, kw), F32),
        ],
        compiler_params=pltpu.CompilerParams(
            dimension_semantics=("arbitrary", "arbitrary"),
            vmem_limit_bytes=VMEM_LIMIT_BYTES),
        name="recurrences",
    )(zf3, p3, p3, lb_row, p3, p3, p3, r3, w2_pad, gate_b, incl3, lvl)


def _silu(g):
    return g * _sigmoid(g)


def _head_rmsnorm(o, head_ones, gain):
    sq = o * o
    hi = sq.astype(BF16)
    lo = (sq - hi.astype(F32)).astype(BF16)
    ms = (jnp.dot(hi, head_ones, preferred_element_type=F32)
          + jnp.dot(lo, head_ones, preferred_element_type=F32)) * (1.0 / HEAD_DIM)
    return o * lax.rsqrt(ms + NORM_EPS) * gain


def _merge_kernel(ao_ref, ho_ref, co_ref, ag_ref, hg_ref, cg_ref,
                  ga_ref, gb_ref, gc_ref, x_ref, wup_ref, wout_ref, ones_ref,
                  hgain_ref, cgain_ref, lng_ref, lnb_ref, y_ref):
    head_ones = ones_ref[...]
    chains = [slice(c * MERGE_CHAIN_ROWS, (c + 1) * MERGE_CHAIN_ROWS)
              for c in range(MERGE_ROWS // MERGE_CHAIN_ROWS)]
    gate_refs = (ga_ref, gb_ref, gc_ref)
    ys = []
    for rows in chains:
        y_a = ao_ref[rows, :] * _silu(ag_ref[rows, :].astype(F32))
        y_b = _head_rmsnorm(ho_ref[rows, :], head_ones, hgain_ref[...]) * _silu(
            hg_ref[rows, :].astype(F32))
        y_c = _head_rmsnorm(co_ref[rows, :], head_ones, cgain_ref[...]) * _silu(
            cg_ref[rows, :].astype(F32))
        ys.append((y_a.astype(BF16), y_b.astype(BF16), y_c.astype(BF16)))
    merged = [None] * len(chains)
    for n in range(3):
        for ci, rows in enumerate(chains):
            up = jnp.dot(ys[ci][n], wup_ref[n], preferred_element_type=F32)
            term = _sigmoid(gate_refs[n][rows, :]) * up.astype(BF16)
            merged[ci] = term if n == 0 else merged[ci] + term
    outs = [jnp.dot(m, wout_ref[...], preferred_element_type=F32) for m in merged]
    for rows, out in zip(chains, outs):
        r = DEEPNORM_ALPHA * x_ref[rows, :] + out
        mu = jnp.mean(r, axis=-1, keepdims=True)
        d = r - mu
        var = jnp.mean(d * d, axis=-1, keepdims=True)
        y_ref[rows, :] = d * lax.rsqrt(var + NORM_EPS) * lng_ref[...] + lnb_ref[...]


MERGE_ROWS = 512
MERGE_CHAIN_ROWS = 256


def _merge(a_out, h_out, c_out, p, x2d, w_up, w_out, head_ones, hgain, cgain, ln_g, ln_b,
           layer):
    n = x2d.shape[0]
    tn = MERGE_ROWS
    w = BRANCH_WIDTH
    row_w = lambda cb: pl.BlockSpec((tn, w), lambda i: (i, cb))
    row_d = lambda cb: pl.BlockSpec((tn, D_MODEL), lambda i: (i, cb))
    shared = lambda shape: pl.BlockSpec(shape, lambda i: tuple(0 for _ in shape))
    const = lambda shape: pl.BlockSpec((None,) + shape,
                                       lambda i: (layer,) + tuple(0 for _ in shape))
    gate0 = COL_GATES // D_MODEL
    return pl.pallas_call(
        _merge_kernel,
        grid=(n // tn,),
        in_specs=[
            row_w(0), row_w(0), row_w(0),
            row_w((COL_A + 3 * w) // w), row_w((COL_B + 3 * w) // w), row_w(COL_C_G // w),
            row_d(gate0), row_d(gate0 + 1), row_d(gate0 + 2),
            row_d(0),
            const((3, w, D_MODEL)), const((D_MODEL, D_MODEL)), shared((w, w)),
            const((1, w)), const((1, w)), const((1, D_MODEL)), const((1, D_MODEL)),
        ],
        out_specs=pl.BlockSpec((tn, D_MODEL), lambda i: (i, 0)),
        out_shape=jax.ShapeDtypeStruct((n, D_MODEL), F32),
        compiler_params=pltpu.CompilerParams(
            dimension_semantics=("arbitrary",),
            vmem_limit_bytes=VMEM_LIMIT_BYTES),
        name="merge_out_norm",
    )(a_out, h_out, c_out, p, p, p, p, p, p, x2d, w_up, w_out, head_ones,
      hgain, cgain, ln_g, ln_b)


SRC_C_Q = 2 * D_MODEL
SRC_C_K = SRC_C_Q + GLA_KEY_WIDTH
SRC_C_V = SRC_C_K + GLA_KEY_WIDTH
SRC_C_G = SRC_C_V + BRANCH_WIDTH
SRC_C_R = SRC_C_G + BRANCH_WIDTH
SRC_GATES = SRC_C_R + GLA_GATE_RANK
N_GATE_TILES = 3 * D_MODEL // LANES
PACK_ROWS = 256


def _pack_kernel(w_ref, s_main_ref, s_edge_ref, o_ref):
    def copy(dst, src, width):
        o_ref[:, dst:dst + width] = w_ref[:, src:src + width].astype(BF16)

    copy(COL_A, 0, 2 * D_MODEL)
    copy(COL_C_V, SRC_C_V, BRANCH_WIDTH)
    copy(COL_C_G, SRC_C_G, BRANCH_WIDTH)
    copy(COL_C_Q, SRC_C_Q, GLA_KEY_WIDTH)
    copy(COL_C_K, SRC_C_K, GLA_KEY_WIDTH)
    lane = lax.broadcasted_iota(jnp.int32, (1, LANES), 1)
    o_ref[:, COL_C_R:COL_C_R + LANES] = jnp.where(
        lane < GLA_GATE_RANK, w_ref[:, SRC_C_R:SRC_C_R + LANES], 0.0).astype(BF16)
    o_ref[:, COL_C_R + LANES:IN_PACKED] = jnp.zeros(
        (o_ref.shape[0], IN_PACKED - COL_C_R - LANES), BF16)
    s_main = s_main_ref[...]
    s_edge = s_edge_ref[...]
    for t in range(N_GATE_TILES):
        src = SRC_C_R + t * LANES
        main = w_ref[:, src:src + LANES].astype(BF16)
        edge = w_ref[:, src + LANES:src + LANES + GLA_GATE_RANK].astype(BF16)
        tile = (jnp.dot(main, s_main, preferred_element_type=F32)
                + jnp.dot(edge, s_edge, preferred_element_type=F32))
        o_ref[:, COL_GATES + t * LANES:COL_GATES + (t + 1) * LANES] = tile.astype(BF16)


def _pack_w_in(w_in):
    depth, d, total = w_in.shape
    i = np.arange(LANES)
    s_main = jnp.asarray((i[:, None] == i[None, :] + GLA_GATE_RANK).astype(np.float32), BF16)
    s_edge = jnp.asarray((i[:GLA_GATE_RANK, None] + LANES == i[None, :] + GLA_GATE_RANK)
                         .astype(np.float32), BF16)
    return pl.pallas_call(
        _pack_kernel,
        grid=(depth, d // PACK_ROWS),
        in_specs=[
            pl.BlockSpec((None, PACK_ROWS, total), lambda l, i: (l, i, 0)),
            pl.BlockSpec(s_main.shape, lambda l, i: (0, 0)),
            pl.BlockSpec(s_edge.shape, lambda l, i: (0, 0)),
        ],
        out_specs=pl.BlockSpec((None, PACK_ROWS, IN_PACKED), lambda l, i: (l, i, 0)),
        out_shape=jax.ShapeDtypeStruct((depth, d, IN_PACKED), BF16),
        compiler_params=pltpu.CompilerParams(
            dimension_semantics=("arbitrary", "arbitrary"),
            vmem_limit_bytes=VMEM_LIMIT_BYTES),
        name="pack_w_in",
    )(w_in, s_main, s_edge)


def kernel(x, w_in, gla_gate_w2, gla_gate_b, hgrn_lb_logits, hgrn_norm_g, gla_norm_g,
           w_up, w_out, ln_g, ln_b):
    batch, seq, d = x.shape
    n = batch * seq
    lb_soft = jax.nn.softmax(hgrn_lb_logits.astype(F32), axis=0)
    lower_bounds = jnp.cumsum(lb_soft, axis=0) - lb_soft[0:1]
    lvl, incl3 = _recurrence_constants()
    hd = np.arange(BRANCH_WIDTH) // HEAD_DIM
    head_ones = jnp.asarray((hd[:, None] == hd[None, :]).astype(np.float32), dtype=BF16)

    w_packed = _pack_w_in(w_in)
    w2_pad = jnp.zeros((DEPTH, GLA_KEY_WIDTH, GLA_KEY_WIDTH), BF16).at[:, :GLA_GATE_RANK].set(
        gla_gate_w2.astype(BF16))
    w_up_bf = w_up.astype(BF16)
    w_out_bf = w_out.astype(BF16)
    rows = lambda a: a.astype(F32)[:, None, :]

    x2d = x.reshape(n, d)
    for layer in range(DEPTH):
        p, zf, r = _inproj(x2d, w_packed, layer)
        p3 = p.reshape(batch, seq, IN_PACKED)
        a_out = _attention(p3)
        h_out, c_out = _recurrences(
            p3, zf.reshape(batch, seq, -1), r.reshape(batch, seq, -1),
            rows(lower_bounds), w2_pad, rows(gla_gate_b), lvl, incl3, layer)
        x2d = _merge(a_out.reshape(n, -1), h_out.reshape(n, -1), c_out.reshape(n, -1),
                     p, x2d, w_up_bf, w_out_bf, head_ones,
                     rows(hgrn_norm_g), rows(gla_norm_g), rows(ln_g), rows(ln_b), layer)
    return x2d.reshape(batch, seq, d)
```

```python
import itertools

import numpy as np
import jax
import jax.numpy as jnp
from jax import lax
from jax.experimental import pallas as pl
from jax.experimental.pallas import tpu as pltpu

F32 = jnp.float32
BF16 = jnp.bfloat16

D_MODEL = 1024
DEPTH = 2
BRANCH_WIDTH = D_MODEL // 4
N_HEADS = 4
HEAD_DIM = BRANCH_WIDTH // N_HEADS
GLA_KEY_DIM = HEAD_DIM // 2
GLA_KEY_WIDTH = N_HEADS * GLA_KEY_DIM
GLA_GATE_RANK = 16
GLA_TAU = 16.0
HGRN_EXP_CLIP = 60.0
NORM_EPS = 1e-5
DEEPNORM_ALPHA = (2 * DEPTH) ** 0.25

LANES = 128
VMEM_LIMIT_BYTES = 56 * 1024 * 1024

COL_A = 0
COL_B = 1024
COL_GATES = 2048
COL_C_V = 5120
COL_C_G = 5376
COL_C_Q = 5632
COL_C_K = 5760
COL_C_R = 5888
IN_PACKED = 6144

CHUNK = 64
N_LEVELS = 6
ATTN_BLOCK = 256
ATTN_STEPS_PER_ITER = 4
LOG2E = 1.4426950408889634


def _nt_dot(a, b):
    return lax.dot_general(a, b, (((1,), (1,)), ((), ())), preferred_element_type=F32)


def _tn_dot(a, b):
    return lax.dot_general(a, b, (((0,), (0,)), ((), ())), preferred_element_type=F32)


def _neg_softplus_neg_abs(z):
    return -jnp.log(1.0 + jnp.exp(-jnp.abs(z)))


def _log_sigmoid(z):
    return jnp.minimum(z, 0.0) + _neg_softplus_neg_abs(z)


def _sigmoid(z):
    return 1.0 / (1.0 + jnp.exp(-z))


INPROJ_BM = 2048
INPROJ_BN = 1024


def _inproj_kernel(x_ref, w_ref, p_ref, zf_ref, r_ref, vt_ref):
    j = pl.program_id(1)
    acc = _nt_dot(x_ref[...].astype(BF16), w_ref[...])
    col = j * INPROJ_BN + lax.broadcasted_iota(jnp.int32, (1, INPROJ_BN), 1)
    scale = jnp.where(col < COL_A + BRANCH_WIDTH, HEAD_DIM ** -0.5, 1.0)
    p_ref[...] = (acc * scale).astype(BF16)

    @pl.when(j == COL_A // INPROJ_BN)
    def _():
        c0 = COL_A % INPROJ_BN + 2 * BRANCH_WIDTH
        for s in range(INPROJ_BM // ATTN_BLOCK):
            v_blk = acc[s * ATTN_BLOCK:(s + 1) * ATTN_BLOCK, c0:c0 + BRANCH_WIDTH]
            vt_ref[s] = v_blk.T.astype(BF16)

    @pl.when(j == COL_B // INPROJ_BN)
    def _():
        c0 = COL_B % INPROJ_BN
        zf_ref[...] = acc[:, c0:c0 + BRANCH_WIDTH]

    @pl.when(j == COL_C_R // INPROJ_BN)
    def _():
        c0 = COL_C_R % INPROJ_BN
        r_ref[...] = acc[:, c0:c0 + LANES]


def _inproj(x2d, w_packed, layer):
    n = x2d.shape[0]
    bm, bn = INPROJ_BM, INPROJ_BN
    return pl.pallas_call(
        _inproj_kernel,
        grid=(n // bm, IN_PACKED // bn),
        in_specs=[
            pl.BlockSpec((bm, D_MODEL), lambda i, j: (i, 0)),
            pl.BlockSpec((None, bn, D_MODEL), lambda i, j: (layer, j, 0)),
        ],
        out_specs=[
            pl.BlockSpec((bm, bn), lambda i, j: (i, j)),
            pl.BlockSpec((bm, BRANCH_WIDTH), lambda i, j: (i, 0)),
            pl.BlockSpec((bm, LANES), lambda i, j: (i, 0)),
            pl.BlockSpec((bm // ATTN_BLOCK, BRANCH_WIDTH, ATTN_BLOCK), lambda i, j: (i, 0, 0)),
        ],
        out_shape=[
            jax.ShapeDtypeStruct((n, IN_PACKED), BF16),
            jax.ShapeDtypeStruct((n, BRANCH_WIDTH), F32),
            jax.ShapeDtypeStruct((n, LANES), F32),
            jax.ShapeDtypeStruct((n // ATTN_BLOCK, BRANCH_WIDTH, ATTN_BLOCK), BF16),
        ],
        compiler_params=pltpu.CompilerParams(
            dimension_semantics=("arbitrary", "arbitrary"),
            vmem_limit_bytes=VMEM_LIMIT_BYTES),
        name="inproj",
    )(x2d, w_packed)


def _attn_kernel(q_ref, k_ref, vt_ref, tri_ref, o_ref,
                 acc_ref, car_ref, z0_ref, z1_ref, w_ref):
    blk = q_ref.shape[1]
    qi = pl.program_id(1)
    lane = lax.broadcasted_iota(jnp.int32, (1, LANES), 1)
    q_heads = []
    for h in range(N_HEADS):
        q_tile = q_ref[0, :, (h // 2) * LANES:(h // 2 + 1) * LANES]
        keep = (lane < HEAD_DIM) if h % 2 == 0 else (lane >= HEAD_DIM)
        q_heads.append(jnp.where(keep, q_tile, jnp.zeros_like(q_tile)))
    key_pos = lax.broadcasted_iota(jnp.int32, (blk, blk), 0)
    query_pos = lax.broadcasted_iota(jnp.int32, (blk, blk), 1)
    causal = key_pos < query_pos

    acc_ref[...] = jnp.zeros_like(acc_ref)
    car_ref[...] = jnp.zeros_like(car_ref)

    def lanes_of(h):
        return slice((h // 2) * LANES, (h // 2 + 1) * LANES)

    def key_block(j):
        return qi - j

    z_refs = (z0_ref, z1_ref)

    def scores_h(h, j, parity):
        rows = pl.ds(pl.multiple_of(key_block(j) * blk, blk), blk)
        z_refs[parity][h] = _nt_dot(k_ref[0, rows, lanes_of(h)], q_heads[h])

    def sums_h(h, parity, diagonal=False, last=False):
        z = z_refs[parity][h]
        softplus = jnp.maximum(z, 0.0) + jnp.log(
            1.0 + jnp.exp2(jnp.abs(z) * (-LOG2E)))
        if diagonal:
            softplus = jnp.where(causal, softplus, 0.0)
        suffix = jnp.dot(tri_ref[...], softplus.astype(BF16), preferred_element_type=F32)
        car = car_ref[h]
        w = jnp.exp(z + suffix + jnp.concatenate([car] * (blk // 8), axis=0))
        if diagonal:
            w = jnp.where(causal, w, 0.0)
        w = w.astype(BF16)
        if last:
            return w
        w_ref[h] = w
        car_ref[h] = car + jnp.broadcast_to(suffix[0:1, :], car.shape)
        return None

    def values_h(h, j, w=None):
        vt_tile = vt_ref[key_block(j), h * HEAD_DIM:(h + 1) * HEAD_DIM, :]
        w = w_ref[h] if w is None else w
        acc_ref[h] += jnp.dot(vt_tile, w, preferred_element_type=F32)

    def step(t, parity):
        for h in range(N_HEADS):
            scores_h(h, t, parity)
            values_h(h, t - 2)
            sums_h(h, 1 - parity)

    def drain(parity):
        for h in range(N_HEADS):
            values_h(h, qi - 1)
            values_h(h, qi, sums_h(h, parity, last=True))

    for h in range(N_HEADS):
        scores_h(h, 0, 0)

    @pl.when(qi == 0)
    def _():
        for h in range(N_HEADS):
            values_h(h, 0, sums_h(h, 0, True, last=True))

    @pl.when(qi == 1)
    def _():
        for h in range(N_HEADS):
            scores_h(h, 1, 1)
            sums_h(h, 0, True)
        drain(1)

    @pl.when(qi >= 2)
    def _():
        for h in range(N_HEADS):
            scores_h(h, 1, 1)
            sums_h(h, 0, True)
        step(2, 0)

        n_steady = qi - 2
        pairs_per_iter = ATTN_STEPS_PER_ITER // 2

        def body(i, c):
            t = 3 + ATTN_STEPS_PER_ITER * i
            for n in range(pairs_per_iter):
                step(t + 2 * n, 1)
                step(t + 2 * n + 1, 0)
            return c

        lax.fori_loop(0, n_steady // ATTN_STEPS_PER_ITER, body, 0)
        t_rest = 3 + (n_steady // ATTN_STEPS_PER_ITER) * ATTN_STEPS_PER_ITER
        for n in range(pairs_per_iter - 1):
            @pl.when((n_steady % ATTN_STEPS_PER_ITER) // 2 > n)
            def _():
                step(t_rest + 2 * n, 1)
                step(t_rest + 2 * n + 1, 0)

        @pl.when(qi % 2 == 1)
        def _():
            step(qi, 1)
            drain(1)

        @pl.when(qi % 2 == 0)
        def _():
            drain(0)

    o_ref[0] = acc_ref[...].reshape(N_HEADS * HEAD_DIM, blk).T


def _attention(p3, vt):
    batch, seq, _ = p3.shape
    blk = ATTN_BLOCK
    w = BRANCH_WIDTH
    qkv3 = p3
    j = np.arange(blk)
    tri = jnp.asarray(-(j[None, :] >= j[:, None]).astype(np.float32), dtype=BF16)
    n_kb = seq // blk
    return pl.pallas_call(
        _attn_kernel,
        grid=(batch, seq // blk),
        in_specs=[
            pl.BlockSpec((1, blk, w), lambda b, i: (b, i, 0)),
            pl.BlockSpec((1, seq, w), lambda b, i: (b, 0, 1)),
            pl.BlockSpec((n_kb, w, blk), lambda b, i: (b, 0, 0)),
            pl.BlockSpec((blk, blk), lambda b, i: (0, 0)),
        ],
        out_specs=pl.BlockSpec((1, blk, w), lambda b, i: (b, i, 0)),
        out_shape=jax.ShapeDtypeStruct((batch, seq, w), F32),
        scratch_shapes=[
            pltpu.VMEM((N_HEADS, HEAD_DIM, blk), F32),
            pltpu.VMEM((N_HEADS, 8, blk), F32),
            pltpu.VMEM((N_HEADS, blk, blk), F32),
            pltpu.VMEM((N_HEADS, blk, blk), F32),
            pltpu.VMEM((N_HEADS, blk, blk), BF16),
        ],
        compiler_params=pltpu.CompilerParams(
            dimension_semantics=("arbitrary", "arbitrary"),
            vmem_limit_bytes=VMEM_LIMIT_BYTES),
        name="stickbreak_attn",
    )(qkv3, qkv3, vt, tri)


def _recurrence_constants():
    t = np.arange(CHUNK)
    x = t[:, None] ^ t[None, :]
    top_bit = np.floor(np.log2(np.maximum(x, 1))).astype(np.int32)
    lvl = np.where(t[:, None] > t[None, :], top_bit,
                   np.where(t[:, None] == t[None, :], N_LEVELS, -1)).astype(np.int32)
    lvl_all = np.tile(lvl, (1, N_HEADS))
    incl = (t[None, :] <= t[:, None]).astype(np.float32)
    incl3 = np.concatenate([incl, incl, incl], axis=1)
    return jnp.asarray(lvl_all), jnp.asarray(incl3, dtype=BF16)


def _run_chunks(q_of, k_ref, v_of, lf_ref, incl3_ref, lvl_ref, o_ref, st_ref, cum_ref,
                n_chunks, key_dim):
    c = CHUNK
    incl3 = incl3_ref[...]
    lvl = lvl_ref[...]
    chunks = range(n_chunks)
    q = [q_of(ci * c) for ci in chunks]
    k = [k_ref[ci * c:(ci + 1) * c, :] for ci in chunks]
    v_bf = [v_of(ci * c) for ci in chunks]
    lf = [lf_ref[ci * c:(ci + 1) * c, :] for ci in chunks]
    wk = q[0].shape[1]
    wv = v_bf[0].shape[1]

    lane_k = lax.broadcasted_iota(jnp.int32, (1, wk), 1)
    lane_v = lax.broadcasted_iota(jnp.int32, (1, wv), 1)
    row = lax.broadcasted_iota(jnp.int32, (c, 1), 0)
    same_head = (lax.broadcasted_iota(jnp.int32, (wv, wk), 0) // HEAD_DIM
                 == lax.broadcasted_iota(jnp.int32, (wv, wk), 1) // key_dim)

    def block_diag_rows(a, lane, width):
        zero = jnp.zeros_like(a)
        return jnp.concatenate(
            [jnp.where((lane >= h * width) & (lane < (h + 1) * width), a, zero)
             for h in range(N_HEADS)], axis=0)

    cum = []
    for ci in chunks:
        h1 = lf[ci].astype(BF16)
        r1 = lf[ci] - h1.astype(F32)
        h2 = r1.astype(BF16)
        h3 = (r1 - h2.astype(F32)).astype(BF16)
        cum.append(jnp.dot(incl3, jnp.concatenate([h1, h2, h3], axis=0),
                           preferred_element_type=F32))
        cum_ref[ci] = cum[ci]
    last = [cum[ci][c - 1:c, :] for ci in chunks]
    yield

    upd = [jnp.dot(v_bf[ci].astype(F32).T.astype(BF16),
                   (k[ci] * jnp.exp(last[ci] - cum[ci])).astype(BF16),
                   preferred_element_type=F32)
           for ci in chunks]
    yield

    scores = [jnp.zeros((c, N_HEADS * c), F32) for _ in chunks]
    for level in range(N_LEVELS + 1):
        b = 1 << level
        for ci in chunks:
            if level == N_LEVELS:
                ql, kl = q[ci], k[ci]
            else:
                if level == 0:
                    arg = jnp.where((row & 1) == 1, lf[ci], 0.0)
                elif level == 1:
                    mids = [jnp.where((row[:8] & 4) == 0,
                                      jnp.broadcast_to(cum_ref[ci, m + 1:m + 2, :], (8, wk)),
                                      jnp.broadcast_to(cum_ref[ci, m + 5:m + 6, :], (8, wk)))
                            for m in range(0, c, 8)]
                    arg = -jnp.abs(cum[ci] - jnp.concatenate(mids, axis=0))
                else:
                    mids = [jnp.broadcast_to(cum_ref[ci, m:m + 1, :], (2 * b, wk))
                            for m in range(b - 1, c, 2 * b)]
                    mid = mids[0] if len(mids) == 1 else jnp.concatenate(mids, axis=0)
                    arg = -jnp.abs(cum[ci] - mid)
                e = jnp.exp(arg)
                ql, kl = q[ci] * e, k[ci] * e
            s_l = _nt_dot(ql.astype(BF16),
                          block_diag_rows(kl.astype(BF16), lane_k, key_dim))
            scores[ci] = jnp.where(lvl == level, s_l, scores[ci])
        yield

    st = st_ref[...]
    o = []
    for ci in chunks:
        st_bf = st.astype(BF16)
        o.append(_nt_dot((q[ci] * jnp.exp(cum[ci])).astype(BF16),
                         jnp.where(same_head, st_bf, jnp.zeros_like(st_bf))))
        st = jnp.exp(last[ci]) * st + upd[ci]
    st_ref[...] = st
    yield

    for ci in chunks:
        o_ref[0, ci * c:(ci + 1) * c, :] = o[ci] + jnp.dot(
            scores[ci].astype(BF16), block_diag_rows(v_bf[ci], lane_v, HEAD_DIM),
            preferred_element_type=F32)


def _recurrence_kernel(zf_ref, hv_ref, hq_ref, lb_ref,
                       cq_ref, ck_ref, cv_ref, r_ref, w2_ref, b_ref,
                       incl3_ref, lvl_ref, ho_ref, co_ref,
                       hst_ref, hcum_ref, hlf_ref, hk_ref,
                       cst_ref, ccum_ref, clf_ref, ckf_ref, cqs_ref):
    tt = zf_ref.shape[1]

    @pl.when(pl.program_id(1) == 0)
    def _():
        hst_ref[...] = jnp.zeros_like(hst_ref)
        cst_ref[...] = jnp.zeros_like(cst_ref)

    zf = zf_ref[0]
    lb = lb_ref[...]
    hlf_ref[...] = _log_sigmoid(zf) + jnp.log(
        1.0 + lb * jnp.exp(jnp.minimum(-zf, HGRN_EXP_CLIP)))
    hk_ref[...] = (1.0 - lb) * _sigmoid(-zf)

    gate_pre = jnp.dot(r_ref[0].astype(BF16), w2_ref[...],
                       preferred_element_type=F32) + b_ref[...]
    clf_ref[...] = _log_sigmoid(gate_pre) / GLA_TAU
    ckf_ref[...] = ck_ref[0].astype(F32)
    cqs_ref[...] = cq_ref[0].astype(F32) * (GLA_KEY_DIM ** -0.5)

    hgrn = _run_chunks(lambda r0: hq_ref[0, r0:r0 + CHUNK, :].astype(F32), hk_ref,
                       lambda r0: hv_ref[0, r0:r0 + CHUNK, :], hlf_ref, incl3_ref, lvl_ref,
                       ho_ref, hst_ref, hcum_ref, tt // CHUNK, HEAD_DIM)
    gla = _run_chunks(lambda r0: cqs_ref[r0:r0 + CHUNK, :], ckf_ref,
                      lambda r0: cv_ref[0, r0:r0 + CHUNK, :], clf_ref, incl3_ref, lvl_ref,
                      co_ref, cst_ref, ccum_ref, tt // CHUNK, GLA_KEY_DIM)
    for _ in itertools.zip_longest(hgrn, gla):
        pass


_REC_BLOCK = 512


def _const_spec(shape):
    return pl.BlockSpec(shape, lambda b, t: tuple(0 for _ in shape))


def _recurrences(p3, zf3, r3, lb_row, w2_pad, gate_b, lvl, incl3, layer):
    batch, seq, _ = p3.shape
    tt = _REC_BLOCK
    w = BRANCH_WIDTH
    kw = GLA_KEY_WIDTH
    cb = COL_B // w
    layer_spec = lambda shape: pl.BlockSpec(
        (None,) + shape, lambda b, t: (layer,) + tuple(0 for _ in shape))
    out_spec = pl.BlockSpec((1, tt, w), lambda b, t: (b, t, 0))
    out_shape = jax.ShapeDtypeStruct((batch, seq, w), F32)
    return pl.pallas_call(
        _recurrence_kernel,
        grid=(batch, seq // tt),
        in_specs=[
            pl.BlockSpec((1, tt, w), lambda b, t: (b, t, 0)),
            pl.BlockSpec((1, tt, w), lambda b, t: (b, t, cb + 1)),
            pl.BlockSpec((1, tt, w), lambda b, t: (b, t, cb + 2)),
            layer_spec((1, w)),
            pl.BlockSpec((1, tt, kw), lambda b, t: (b, t, COL_C_Q // kw)),
            pl.BlockSpec((1, tt, kw), lambda b, t: (b, t, COL_C_K // kw)),
            pl.BlockSpec((1, tt, w), lambda b, t: (b, t, COL_C_V // w)),
            pl.BlockSpec((1, tt, kw), lambda b, t: (b, t, 0)),
            layer_spec((kw, kw)),
            layer_spec((1, kw)),
            _const_spec(incl3.shape),
            _const_spec(lvl.shape),
        ],
        out_specs=[out_spec, out_spec],
        out_shape=[out_shape, out_shape],
        scratch_shapes=[
            pltpu.VMEM((w, w), F32),
            pltpu.VMEM((tt // CHUNK, CHUNK, w), F32),
            pltpu.VMEM((tt, w), F32),
            pltpu.VMEM((tt, w), F32),
            pltpu.VMEM((w, kw), F32),
            pltpu.VMEM((tt // CHUNK, CHUNK, kw), F32),
            pltpu.VMEM((tt, kw), F32),
            pltpu.VMEM((tt, kw), F32),
            pltpu.VMEM((tt, kw), F32),
        ],
        compiler_params=pltpu.CompilerParams(
            dimension_semantics=("arbitrary", "arbitrary"),
            vmem_limit_bytes=VMEM_LIMIT_BYTES),
        name="recurrences",
    )(zf3, p3, p3, lb_row, p3, p3, p3, r3, w2_pad, gate_b, incl3, lvl)


def _silu(g):
    return g * _sigmoid(g)


def _head_rmsnorm(o, head_ones, gain):
    sq = o * o
    hi = sq.astype(BF16)
    lo = (sq - hi.astype(F32)).astype(BF16)
    ms = (jnp.dot(hi, head_ones, preferred_element_type=F32)
          + jnp.dot(lo, head_ones, preferred_element_type=F32)) * (1.0 / HEAD_DIM)
    return o * lax.rsqrt(ms + NORM_EPS) * gain


def _merge_kernel(ao_ref, ho_ref, co_ref, ag_ref, hg_ref, cg_ref,
                  ga_ref, gb_ref, gc_ref, x_ref, wup_ref, wout_ref, ones_ref,
                  hgain_ref, cgain_ref, lng_ref, lnb_ref, y_ref):
    head_ones = ones_ref[...]
    chains = [slice(c * MERGE_CHAIN_ROWS, (c + 1) * MERGE_CHAIN_ROWS)
              for c in range(MERGE_ROWS // MERGE_CHAIN_ROWS)]
    gate_refs = (ga_ref, gb_ref, gc_ref)
    ys = []
    for rows in chains:
        y_a = ao_ref[rows, :] * _silu(ag_ref[rows, :].astype(F32))
        y_b = _head_rmsnorm(ho_ref[rows, :], head_ones, hgain_ref[...]) * _silu(
            hg_ref[rows, :].astype(F32))
        y_c = _head_rmsnorm(co_ref[rows, :], head_ones, cgain_ref[...]) * _silu(
            cg_ref[rows, :].astype(F32))
        ys.append((y_a.astype(BF16), y_b.astype(BF16), y_c.astype(BF16)))
    merged = [None] * len(chains)
    for n in range(3):
        for ci, rows in enumerate(chains):
            up = jnp.dot(ys[ci][n], wup_ref[n], preferred_element_type=F32)
            term = _sigmoid(gate_refs[n][rows, :]) * up.astype(BF16)
            merged[ci] = term if n == 0 else merged[ci] + term
    outs = [jnp.dot(m, wout_ref[...], preferred_element_type=F32) for m in merged]
    for rows, out in zip(chains, outs):
        r = DEEPNORM_ALPHA * x_ref[rows, :] + out
        mu = jnp.mean(r, axis=-1, keepdims=True)
        d = r - mu
        var = jnp.mean(d * d, axis=-1, keepdims=True)
        y_ref[rows, :] = d * lax.rsqrt(var + NORM_EPS) * lng_ref[...] + lnb_ref[...]


MERGE_ROWS = 512
MERGE_CHAIN_ROWS = 256


def _merge(a_out, h_out, c_out, p, x2d, w_up, w_out, head_ones, hgain, cgain, ln_g, ln_b,
           layer):
    n = x2d.shape[0]
    tn = MERGE_ROWS
    w = BRANCH_WIDTH
    row_w = lambda cb: pl.BlockSpec((tn, w), lambda i: (i, cb))
    row_d = lambda cb: pl.BlockSpec((tn, D_MODEL), lambda i: (i, cb))
    shared = lambda shape: pl.BlockSpec(shape, lambda i: tuple(0 for _ in shape))
    const = lambda shape: pl.BlockSpec((None,) + shape,
                                       lambda i: (layer,) + tuple(0 for _ in shape))
    gate0 = COL_GATES // D_MODEL
    return pl.pallas_call(
        _merge_kernel,
        grid=(n // tn,),
        in_specs=[
            row_w(0), row_w(0), row_w(0),
            row_w((COL_A + 3 * w) // w), row_w((COL_B + 3 * w) // w), row_w(COL_C_G // w),
            row_d(gate0), row_d(gate0 + 1), row_d(gate0 + 2),
            row_d(0),
            const((3, w, D_MODEL)), const((D_MODEL, D_MODEL)), shared((w, w)),
            const((1, w)), const((1, w)), const((1, D_MODEL)), const((1, D_MODEL)),
        ],
        out_specs=pl.BlockSpec((tn, D_MODEL), lambda i: (i, 0)),
        out_shape=jax.ShapeDtypeStruct((n, D_MODEL), F32),
        compiler_params=pltpu.CompilerParams(
            dimension_semantics=("arbitrary",),
            vmem_limit_bytes=VMEM_LIMIT_BYTES),
        name="merge_out_norm",
    )(a_out, h_out, c_out, p, p, p, p, p, p, x2d, w_up, w_out, head_ones,
      hgain, cgain, ln_g, ln_b)


SRC_C_Q = 2 * D_MODEL
SRC_C_K = SRC_C_Q + GLA_KEY_WIDTH
SRC_C_V = SRC_C_K + GLA_KEY_WIDTH
SRC_C_G = SRC_C_V + BRANCH_WIDTH
SRC_C_R = SRC_C_G + BRANCH_WIDTH
SRC_GATES = SRC_C_R + GLA_GATE_RANK
PACK_LANES = 256


def _pack_kernel(w_ref, o_ref):
    def copy(dst, src, rows):
        o_ref[dst:dst + rows, :] = w_ref[src:src + rows, :].astype(BF16)

    copy(COL_A, 0, 2 * D_MODEL)
    copy(COL_GATES, SRC_GATES, 3 * D_MODEL)
    copy(COL_C_V, SRC_C_V, BRANCH_WIDTH)
    copy(COL_C_G, SRC_C_G, BRANCH_WIDTH)
    copy(COL_C_Q, SRC_C_Q, GLA_KEY_WIDTH)
    copy(COL_C_K, SRC_C_K, GLA_KEY_WIDTH)
    copy(COL_C_R, SRC_C_R, GLA_GATE_RANK)
    pad0 = COL_C_R + GLA_GATE_RANK
    o_ref[pad0:IN_PACKED, :] = jnp.zeros((IN_PACKED - pad0, o_ref.shape[1]), BF16)


def _pack_w_in(w_in):
    depth, d, total = w_in.shape
    w_t = jnp.swapaxes(w_in, 1, 2)
    return pl.pallas_call(
        _pack_kernel,
        grid=(depth, d // PACK_LANES),
        in_specs=[pl.BlockSpec((None, total, PACK_LANES), lambda l, i: (l, 0, i))],
        out_specs=pl.BlockSpec((None, IN_PACKED, PACK_LANES), lambda l, i: (l, 0, i)),
        out_shape=jax.ShapeDtypeStruct((depth, IN_PACKED, d), BF16),
        compiler_params=pltpu.CompilerParams(
            dimension_semantics=("arbitrary", "arbitrary"),
            vmem_limit_bytes=VMEM_LIMIT_BYTES),
        name="pack_w_in",
    )(w_t)


def kernel(x, w_in, gla_gate_w2, gla_gate_b, hgrn_lb_logits, hgrn_norm_g, gla_norm_g,
           w_up, w_out, ln_g, ln_b):
    batch, seq, d = x.shape
    n = batch * seq
    lb_soft = jax.nn.softmax(hgrn_lb_logits.astype(F32), axis=0)
    lower_bounds = jnp.cumsum(lb_soft, axis=0) - lb_soft[0:1]
    lvl, incl3 = _recurrence_constants()
    hd = np.arange(BRANCH_WIDTH) // HEAD_DIM
    head_ones = jnp.asarray((hd[:, None] == hd[None, :]).astype(np.float32), dtype=BF16)

    w_packed = _pack_w_in(w_in)
    w2_pad = jnp.zeros((DEPTH, GLA_KEY_WIDTH, GLA_KEY_WIDTH), BF16).at[:, :GLA_GATE_RANK].set(
        gla_gate_w2.astype(BF16))
    w_up_bf = w_up.astype(BF16)
    w_out_bf = w_out.astype(BF16)
    rows = lambda a: a.astype(F32)[:, None, :]

    x2d = x.reshape(n, d)
    for layer in range(DEPTH):
        p, zf, r, vt = _inproj(x2d, w_packed, layer)
        p3 = p.reshape(batch, seq, IN_PACKED)
        a_out = _attention(p3, vt)
        h_out, c_out = _recurrences(
            p3, zf.reshape(batch, seq, -1), r.reshape(batch, seq, -1),
            rows(lower_bounds), w2_pad, rows(gla_gate_b), lvl, incl3, layer)
        x2d = _merge(a_out.reshape(n, -1), h_out.reshape(n, -1), c_out.reshape(n, -1),
                     p, x2d, w_up_bf, w_out_bf, head_ones,
                     rows(hgrn_norm_g), rows(gla_norm_g), rows(ln_g), rows(ln_b), layer)
    return x2d.reshape(batch, seq, d)
```

```python
import itertools

import numpy as np
import jax
import jax.numpy as jnp
from jax import lax
from jax.experimental import pallas as pl
from jax.experimental.pallas import tpu as pltpu

F32 = jnp.float32
BF16 = jnp.bfloat16

D_MODEL = 1024
DEPTH = 2
BRANCH_WIDTH = D_MODEL // 4
N_HEADS = 4
HEAD_DIM = BRANCH_WIDTH // N_HEADS
GLA_KEY_DIM = HEAD_DIM // 2
GLA_KEY_WIDTH = N_HEADS * GLA_KEY_DIM
GLA_GATE_RANK = 16
GLA_TAU = 16.0
HGRN_EXP_CLIP = 60.0
NORM_EPS = 1e-5
DEEPNORM_ALPHA = (2 * DEPTH) ** 0.25

LANES = 128
VMEM_LIMIT_BYTES = 56 * 1024 * 1024

COL_A = 0
COL_B = 1024
COL_GATES = 2048
COL_C_V = 5120
COL_C_G = 5376
COL_C_Q = 5632
COL_C_K = 5760
COL_C_R = 5888
IN_PACKED = 6144

CHUNK = 64
N_LEVELS = 6
ATTN_BLOCK = 256
ATTN_STEPS_PER_ITER = 4
LOG2E = 1.4426950408889634


def _nt_dot(a, b):
    return lax.dot_general(a, b, (((1,), (1,)), ((), ())), preferred_element_type=F32)


def _neg_softplus_neg_abs(z):
    return -jnp.log(1.0 + jnp.exp(-jnp.abs(z)))


def _log_sigmoid(z):
    return jnp.minimum(z, 0.0) + _neg_softplus_neg_abs(z)


def _sigmoid(z):
    return 1.0 / (1.0 + jnp.exp(-z))


INPROJ_BM = 2048
INPROJ_BN = 1024


def _inproj_kernel(x_ref, w_ref, p_ref, zf_ref, r_ref, vt_ref):
    j = pl.program_id(1)
    acc = _nt_dot(x_ref[...].astype(BF16), w_ref[...])
    col = j * INPROJ_BN + lax.broadcasted_iota(jnp.int32, (1, INPROJ_BN), 1)
    scale = jnp.where(col < COL_A + BRANCH_WIDTH, HEAD_DIM ** -0.5, 1.0)
    p_ref[...] = (acc * scale).astype(BF16)

    @pl.when(j == COL_A // INPROJ_BN)
    def _():
        c0 = COL_A % INPROJ_BN + 2 * BRANCH_WIDTH
        for s in range(INPROJ_BM // ATTN_BLOCK):
            v_blk = acc[s * ATTN_BLOCK:(s + 1) * ATTN_BLOCK, c0:c0 + BRANCH_WIDTH]
            vt_ref[s] = v_blk.T.astype(BF16)

    @pl.when(j == COL_B // INPROJ_BN)
    def _():
        c0 = COL_B % INPROJ_BN
        zf_ref[...] = acc[:, c0:c0 + BRANCH_WIDTH]

    @pl.when(j == COL_C_R // INPROJ_BN)
    def _():
        c0 = COL_C_R % INPROJ_BN
        r_ref[...] = acc[:, c0:c0 + LANES]


def _inproj(x2d, w_packed, layer):
    n = x2d.shape[0]
    bm, bn = INPROJ_BM, INPROJ_BN
    return pl.pallas_call(
        _inproj_kernel,
        grid=(n // bm, IN_PACKED // bn),
        in_specs=[
            pl.BlockSpec((bm, D_MODEL), lambda i, j: (i, 0)),
            pl.BlockSpec((None, bn, D_MODEL), lambda i, j: (layer, j, 0)),
        ],
        out_specs=[
            pl.BlockSpec((bm, bn), lambda i, j: (i, j)),
            pl.BlockSpec((bm, BRANCH_WIDTH), lambda i, j: (i, 0)),
            pl.BlockSpec((bm, LANES), lambda i, j: (i, 0)),
            pl.BlockSpec((bm // ATTN_BLOCK, BRANCH_WIDTH, ATTN_BLOCK), lambda i, j: (i, 0, 0)),
        ],
        out_shape=[
            jax.ShapeDtypeStruct((n, IN_PACKED), BF16),
            jax.ShapeDtypeStruct((n, BRANCH_WIDTH), F32),
            jax.ShapeDtypeStruct((n, LANES), F32),
            jax.ShapeDtypeStruct((n // ATTN_BLOCK, BRANCH_WIDTH, ATTN_BLOCK), BF16),
        ],
        compiler_params=pltpu.CompilerParams(
            dimension_semantics=("arbitrary", "arbitrary"),
            vmem_limit_bytes=VMEM_LIMIT_BYTES),
        name="inproj",
    )(x2d, w_packed)


def _attn_kernel(q_ref, k_ref, vt_ref, tri_ref, o_ref,
                 acc_ref, car_ref, z0_ref, z1_ref, w_ref):
    blk = q_ref.shape[1]
    qi = pl.program_id(1)
    lane = lax.broadcasted_iota(jnp.int32, (1, LANES), 1)
    q_heads = []
    for h in range(N_HEADS):
        q_tile = q_ref[0, :, (h // 2) * LANES:(h // 2 + 1) * LANES]
        keep = (lane < HEAD_DIM) if h % 2 == 0 else (lane >= HEAD_DIM)
        q_heads.append(jnp.where(keep, q_tile, jnp.zeros_like(q_tile)))
    key_pos = lax.broadcasted_iota(jnp.int32, (blk, blk), 0)
    query_pos = lax.broadcasted_iota(jnp.int32, (blk, blk), 1)
    causal = key_pos < query_pos

    acc_ref[...] = jnp.zeros_like(acc_ref)
    car_ref[...] = jnp.zeros_like(car_ref)

    def lanes_of(h):
        return slice((h // 2) * LANES, (h // 2 + 1) * LANES)

    def key_block(j):
        return qi - j

    z_refs = (z0_ref, z1_ref)

    def scores_h(h, j, parity):
        rows = pl.ds(pl.multiple_of(key_block(j) * blk, blk), blk)
        z_refs[parity][h] = _nt_dot(k_ref[0, rows, lanes_of(h)], q_heads[h])

    def sums_h(h, parity, diagonal=False, last=False):
        z = z_refs[parity][h]
        softplus = jnp.maximum(z, 0.0) + jnp.log(
            1.0 + jnp.exp2(jnp.abs(z) * (-LOG2E)))
        if diagonal:
            softplus = jnp.where(causal, softplus, 0.0)
        suffix = jnp.dot(tri_ref[...], softplus.astype(BF16), preferred_element_type=F32)
        car = car_ref[h]
        w = jnp.exp(z + suffix + jnp.concatenate([car] * (blk // 8), axis=0))
        if diagonal:
            w = jnp.where(causal, w, 0.0)
        w = w.astype(BF16)
        if last:
            return w
        w_ref[h] = w
        car_ref[h] = car + jnp.broadcast_to(suffix[0:1, :], car.shape)
        return None

    def values_h(h, j, w=None):
        vt_tile = vt_ref[key_block(j), h * HEAD_DIM:(h + 1) * HEAD_DIM, :]
        w = w_ref[h] if w is None else w
        acc_ref[h] += jnp.dot(vt_tile, w, preferred_element_type=F32)

    def step(t, parity):
        for h in range(N_HEADS):
            scores_h(h, t, parity)
            values_h(h, t - 2)
            sums_h(h, 1 - parity)

    def drain(parity):
        for h in range(N_HEADS):
            values_h(h, qi - 1)
            values_h(h, qi, sums_h(h, parity, last=True))

    def fill():
        for h in range(N_HEADS):
            scores_h(h, 0, 0)
            scores_h(h, 1, 1)
        for h in range(N_HEADS):
            sums_h(h, 0, True)

    @pl.when(qi == 0)
    def _():
        for h in range(N_HEADS):
            scores_h(h, 0, 0)
        for h in range(N_HEADS):
            values_h(h, 0, sums_h(h, 0, True, last=True))

    @pl.when(qi == 1)
    def _():
        fill()
        drain(1)

    @pl.when(qi >= 2)
    def _():
        fill()
        step(2, 0)

        n_steady = qi - 2
        pairs_per_iter = ATTN_STEPS_PER_ITER // 2

        def body(i, c):
            t = 3 + ATTN_STEPS_PER_ITER * i
            for n in range(pairs_per_iter):
                step(t + 2 * n, 1)
                step(t + 2 * n + 1, 0)
            return c

        lax.fori_loop(0, n_steady // ATTN_STEPS_PER_ITER, body, 0)
        t_rest = 3 + (n_steady // ATTN_STEPS_PER_ITER) * ATTN_STEPS_PER_ITER
        for n in range(pairs_per_iter - 1):
            @pl.when((n_steady % ATTN_STEPS_PER_ITER) // 2 > n)
            def _():
                step(t_rest + 2 * n, 1)
                step(t_rest + 2 * n + 1, 0)

        @pl.when(qi % 2 == 1)
        def _():
            step(qi, 1)
            drain(1)

        @pl.when(qi % 2 == 0)
        def _():
            drain(0)

    o_ref[0] = acc_ref[...].reshape(N_HEADS * HEAD_DIM, blk).T


def _attention(p3, vt):
    batch, seq, _ = p3.shape
    blk = ATTN_BLOCK
    w = BRANCH_WIDTH
    qkv3 = p3
    j = np.arange(blk)
    tri = jnp.asarray(-(j[None, :] >= j[:, None]).astype(np.float32), dtype=BF16)
    n_kb = seq // blk
    return pl.pallas_call(
        _attn_kernel,
        grid=(batch, seq // blk),
        in_specs=[
            pl.BlockSpec((1, blk, w), lambda b, i: (b, i, 0)),
            pl.BlockSpec((1, seq, w), lambda b, i: (b, 0, 1)),
            pl.BlockSpec((n_kb, w, blk), lambda b, i: (b, 0, 0)),
            pl.BlockSpec((blk, blk), lambda b, i: (0, 0)),
        ],
        out_specs=pl.BlockSpec((1, blk, w), lambda b, i: (b, i, 0)),
        out_shape=jax.ShapeDtypeStruct((batch, seq, w), F32),
        scratch_shapes=[
            pltpu.VMEM((N_HEADS, HEAD_DIM, blk), F32),
            pltpu.VMEM((N_HEADS, 8, blk), F32),
            pltpu.VMEM((N_HEADS, blk, blk), F32),
            pltpu.VMEM((N_HEADS, blk, blk), F32),
            pltpu.VMEM((N_HEADS, blk, blk), BF16),
        ],
        compiler_params=pltpu.CompilerParams(
            dimension_semantics=("arbitrary", "arbitrary"),
            vmem_limit_bytes=VMEM_LIMIT_BYTES),
        name="stickbreak_attn",
    )(qkv3, qkv3, vt, tri)


def _recurrence_constants():
    t = np.arange(CHUNK)
    x = t[:, None] ^ t[None, :]
    top_bit = np.floor(np.log2(np.maximum(x, 1))).astype(np.int32)
    lvl = np.where(t[:, None] > t[None, :], top_bit,
                   np.where(t[:, None] == t[None, :], N_LEVELS, -1)).astype(np.int32)
    lvl_all = np.tile(lvl, (1, N_HEADS))
    incl = (t[None, :] <= t[:, None]).astype(np.float32)
    incl3 = np.concatenate([incl, incl, incl], axis=1)
    return jnp.asarray(lvl_all), jnp.asarray(incl3, dtype=BF16)


def _run_chunks(q_of, k_ref, v_of, lf_ref, incl3_ref, lvl_ref, o_ref, st_ref, cum_ref,
                n_chunks, key_dim):
    c = CHUNK
    incl3 = incl3_ref[...]
    lvl = lvl_ref[...]
    chunks = range(n_chunks)
    q = [q_of(ci * c) for ci in chunks]
    k = [k_ref[ci * c:(ci + 1) * c, :] for ci in chunks]
    v_bf = [v_of(ci * c) for ci in chunks]
    lf = [lf_ref[ci * c:(ci + 1) * c, :] for ci in chunks]
    wk = q[0].shape[1]
    wv = v_bf[0].shape[1]

    lane_k = lax.broadcasted_iota(jnp.int32, (1, wk), 1)
    lane_v = lax.broadcasted_iota(jnp.int32, (1, wv), 1)
    row = lax.broadcasted_iota(jnp.int32, (c, 1), 0)
    same_head = (lax.broadcasted_iota(jnp.int32, (wv, wk), 0) // HEAD_DIM
                 == lax.broadcasted_iota(jnp.int32, (wv, wk), 1) // key_dim)

    def block_diag_rows(a, lane, width):
        zero = jnp.zeros_like(a)
        return jnp.concatenate(
            [jnp.where((lane >= h * width) & (lane < (h + 1) * width), a, zero)
             for h in range(N_HEADS)], axis=0)

    cum = []
    for ci in chunks:
        h1 = lf[ci].astype(BF16)
        r1 = lf[ci] - h1.astype(F32)
        h2 = r1.astype(BF16)
        h3 = (r1 - h2.astype(F32)).astype(BF16)
        cum.append(jnp.dot(incl3, jnp.concatenate([h1, h2, h3], axis=0),
                           preferred_element_type=F32))
        cum_ref[ci] = cum[ci]
    last = [cum[ci][c - 1:c, :] for ci in chunks]
    yield

    upd = [jnp.dot(v_bf[ci].astype(F32).T.astype(BF16),
                   (k[ci] * jnp.exp(last[ci] - cum[ci])).astype(BF16),
                   preferred_element_type=F32)
           for ci in chunks]
    yield

    scores = [jnp.zeros((c, N_HEADS * c), F32) for _ in chunks]
    for level in range(N_LEVELS + 1):
        b = 1 << level
        for ci in chunks:
            if level == N_LEVELS:
                ql, kl = q[ci], k[ci]
            else:
                if level == 0:
                    arg = jnp.where((row & 1) == 1, lf[ci], 0.0)
                elif level == 1:
                    mids = [jnp.where((row[:8] & 4) == 0,
                                      jnp.broadcast_to(cum_ref[ci, m + 1:m + 2, :], (8, wk)),
                                      jnp.broadcast_to(cum_ref[ci, m + 5:m + 6, :], (8, wk)))
                            for m in range(0, c, 8)]
                    arg = -jnp.abs(cum[ci] - jnp.concatenate(mids, axis=0))
                else:
                    mids = [jnp.broadcast_to(cum_ref[ci, m:m + 1, :], (2 * b, wk))
                            for m in range(b - 1, c, 2 * b)]
                    mid = mids[0] if len(mids) == 1 else jnp.concatenate(mids, axis=0)
                    arg = -jnp.abs(cum[ci] - mid)
                e = jnp.exp(arg)
                ql, kl = q[ci] * e, k[ci] * e
            s_l = _nt_dot(ql.astype(BF16),
                          block_diag_rows(kl.astype(BF16), lane_k, key_dim))
            scores[ci] = jnp.where(lvl == level, s_l, scores[ci])
        yield

    st = st_ref[...]
    o = []
    for ci in chunks:
        st_bf = st.astype(BF16)
        o.append(_nt_dot((q[ci] * jnp.exp(cum[ci])).astype(BF16),
                         jnp.where(same_head, st_bf, jnp.zeros_like(st_bf))))
        st = jnp.exp(last[ci]) * st + upd[ci]
    st_ref[...] = st
    yield

    for ci in chunks:
        o_ref[0, ci * c:(ci + 1) * c, :] = o[ci] + jnp.dot(
            scores[ci].astype(BF16), block_diag_rows(v_bf[ci], lane_v, HEAD_DIM),
            preferred_element_type=F32)


def _recurrence_kernel(zf_ref, hv_ref, hq_ref, lb_ref,
                       cq_ref, ck_ref, cv_ref, r_ref, w2_ref, b_ref,
                       incl3_ref, lvl_ref, ho_ref, co_ref,
                       hst_ref, hcum_ref, hlf_ref, hk_ref,
                       cst_ref, ccum_ref, clf_ref, ckf_ref, cqs_ref):
    tt = zf_ref.shape[1]

    @pl.when(pl.program_id(1) == 0)
    def _():
        hst_ref[...] = jnp.zeros_like(hst_ref)
        cst_ref[...] = jnp.zeros_like(cst_ref)

    zf = zf_ref[0]
    lb = lb_ref[...]
    hlf_ref[...] = _log_sigmoid(zf) + jnp.log(
        1.0 + lb * jnp.exp(jnp.minimum(-zf, HGRN_EXP_CLIP)))
    hk_ref[...] = (1.0 - lb) * _sigmoid(-zf)

    gate_pre = jnp.dot(r_ref[0].astype(BF16), w2_ref[...],
                       preferred_element_type=F32) + b_ref[...]
    clf_ref[...] = _log_sigmoid(gate_pre) / GLA_TAU
    ckf_ref[...] = ck_ref[0].astype(F32)
    cqs_ref[...] = cq_ref[0].astype(F32) * (GLA_KEY_DIM ** -0.5)

    hgrn = _run_chunks(lambda r0: hq_ref[0, r0:r0 + CHUNK, :].astype(F32), hk_ref,
                       lambda r0: hv_ref[0, r0:r0 + CHUNK, :], hlf_ref, incl3_ref, lvl_ref,
                       ho_ref, hst_ref, hcum_ref, tt // CHUNK, HEAD_DIM)
    gla = _run_chunks(lambda r0: cqs_ref[r0:r0 + CHUNK, :], ckf_ref,
                      lambda r0: cv_ref[0, r0:r0 + CHUNK, :], clf_ref, incl3_ref, lvl_ref,
                      co_ref, cst_ref, ccum_ref, tt // CHUNK, GLA_KEY_DIM)
    for _ in itertools.zip_longest(hgrn, gla):
        pass


_REC_BLOCK = 512


def _const_spec(shape):
    return pl.BlockSpec(shape, lambda b, t: tuple(0 for _ in shape))


def _recurrences(p3, zf3, r3, lb_row, w2_pad, gate_b, lvl, incl3, layer):
    batch, seq, _ = p3.shape
    tt = _REC_BLOCK
    w = BRANCH_WIDTH
    kw = GLA_KEY_WIDTH
    cb = COL_B // w
    layer_spec = lambda shape: pl.BlockSpec(
        (None,) + shape, lambda b, t: (layer,) + tuple(0 for _ in shape))
    out_spec = pl.BlockSpec((1, tt, w), lambda b, t: (b, t, 0))
    out_shape = jax.ShapeDtypeStruct((batch, seq, w), F32)
    return pl.pallas_call(
        _recurrence_kernel,
        grid=(batch, seq // tt),
        in_specs=[
            pl.BlockSpec((1, tt, w), lambda b, t: (b, t, 0)),
            pl.BlockSpec((1, tt, w), lambda b, t: (b, t, cb + 1)),
            pl.BlockSpec((1, tt, w), lambda b, t: (b, t, cb + 2)),
            layer_spec((1, w)),
            pl.BlockSpec((1, tt, kw), lambda b, t: (b, t, COL_C_Q // kw)),
            pl.BlockSpec((1, tt, kw), lambda b, t: (b, t, COL_C_K // kw)),
            pl.BlockSpec((1, tt, w), lambda b, t: (b, t, COL_C_V // w)),
            pl.BlockSpec((1, tt, kw), lambda b, t: (b, t, 0)),
            layer_spec((kw, kw)),
            layer_spec((1, kw)),
            _const_spec(incl3.shape),
            _const_spec(lvl.shape),
        ],
        out_specs=[out_spec, out_spec],
        out_shape=[out_shape, out_shape],
        scratch_shapes=[
            pltpu.VMEM((w, w), F32),
            pltpu.VMEM((tt // CHUNK, CHUNK, w), F32),
            pltpu.VMEM((tt, w), F32),
            pltpu.VMEM((tt, w), F32),
            pltpu.VMEM((w, kw), F32),
            pltpu.VMEM((tt // CHUNK, CHUNK, kw), F32),
            pltpu.VMEM((tt, kw), F32),
            pltpu.VMEM((tt, kw), F32),
            pltpu.VMEM((tt, kw), F32),
        ],
        compiler_params=pltpu.CompilerParams(
            dimension_semantics=("arbitrary", "arbitrary"),
            vmem_limit_bytes=VMEM_LIMIT_BYTES),
        name="recurrences",
    )(zf3, p3, p3, lb_row, p3, p3, p3, r3, w2_pad, gate_b, incl3, lvl)


def _silu(g):
    return g * _sigmoid(g)


def _head_rmsnorm(o, head_ones, gain):
    sq = o * o
    hi = sq.astype(BF16)
    lo = (sq - hi.astype(F32)).astype(BF16)
    ms = (jnp.dot(hi, head_ones, preferred_element_type=F32)
          + jnp.dot(lo, head_ones, preferred_element_type=F32)) * (1.0 / HEAD_DIM)
    return o * lax.rsqrt(ms + NORM_EPS) * gain


def _merge_kernel(ao_ref, ho_ref, co_ref, ag_ref, hg_ref, cg_ref,
                  ga_ref, gb_ref, gc_ref, x_ref, wup_ref, wout_ref, ones_ref,
                  hgain_ref, cgain_ref, lng_ref, lnb_ref, y_ref):
    head_ones = ones_ref[...]
    chains = [slice(c * MERGE_CHAIN_ROWS, (c + 1) * MERGE_CHAIN_ROWS)
              for c in range(MERGE_ROWS // MERGE_CHAIN_ROWS)]
    gate_refs = (ga_ref, gb_ref, gc_ref)
    ys = []
    for rows in chains:
        y_a = ao_ref[rows, :] * _silu(ag_ref[rows, :].astype(F32))
        y_b = _head_rmsnorm(ho_ref[rows, :], head_ones, hgain_ref[...]) * _silu(
            hg_ref[rows, :].astype(F32))
        y_c = _head_rmsnorm(co_ref[rows, :], head_ones, cgain_ref[...]) * _silu(
            cg_ref[rows, :].astype(F32))
        ys.append((y_a.astype(BF16), y_b.astype(BF16), y_c.astype(BF16)))
    merged = [None] * len(chains)
    for n in range(3):
        for ci, rows in enumerate(chains):
            up = jnp.dot(ys[ci][n], wup_ref[n], preferred_element_type=F32)
            term = _sigmoid(gate_refs[n][rows, :]) * up.astype(BF16)
            merged[ci] = term if n == 0 else merged[ci] + term
    outs = [jnp.dot(m, wout_ref[...], preferred_element_type=F32) for m in merged]
    for rows, out in zip(chains, outs):
        r = DEEPNORM_ALPHA * x_ref[rows, :] + out
        mu = jnp.mean(r, axis=-1, keepdims=True)
        d = r - mu
        var = jnp.mean(d * d, axis=-1, keepdims=True)
        y_ref[rows, :] = d * lax.rsqrt(var + NORM_EPS) * lng_ref[...] + lnb_ref[...]


MERGE_ROWS = 512
MERGE_CHAIN_ROWS = 256


def _merge(a_out, h_out, c_out, p, x2d, w_up, w_out, head_ones, hgain, cgain, ln_g, ln_b,
           layer):
    n = x2d.shape[0]
    tn = MERGE_ROWS
    w = BRANCH_WIDTH
    row_w = lambda cb: pl.BlockSpec((tn, w), lambda i: (i, cb))
    row_d = lambda cb: pl.BlockSpec((tn, D_MODEL), lambda i: (i, cb))
    shared = lambda shape: pl.BlockSpec(shape, lambda i: tuple(0 for _ in shape))
    const = lambda shape: pl.BlockSpec((None,) + shape,
                                       lambda i: (layer,) + tuple(0 for _ in shape))
    gate0 = COL_GATES // D_MODEL
    return pl.pallas_call(
        _merge_kernel,
        grid=(n // tn,),
        in_specs=[
            row_w(0), row_w(0), row_w(0),
            row_w((COL_A + 3 * w) // w), row_w((COL_B + 3 * w) // w), row_w(COL_C_G // w),
            row_d(gate0), row_d(gate0 + 1), row_d(gate0 + 2),
            row_d(0),
            const((3, w, D_MODEL)), const((D_MODEL, D_MODEL)), shared((w, w)),
            const((1, w)), const((1, w)), const((1, D_MODEL)), const((1, D_MODEL)),
        ],
        out_specs=pl.BlockSpec((tn, D_MODEL), lambda i: (i, 0)),
        out_shape=jax.ShapeDtypeStruct((n, D_MODEL), F32),
        compiler_params=pltpu.CompilerParams(
            dimension_semantics=("arbitrary",),
            vmem_limit_bytes=VMEM_LIMIT_BYTES),
        name="merge_out_norm",
    )(a_out, h_out, c_out, p, p, p, p, p, p, x2d, w_up, w_out, head_ones,
      hgain, cgain, ln_g, ln_b)


SRC_C_Q = 2 * D_MODEL
SRC_C_K = SRC_C_Q + GLA_KEY_WIDTH
SRC_C_V = SRC_C_K + GLA_KEY_WIDTH
SRC_C_G = SRC_C_V + BRANCH_WIDTH
SRC_C_R = SRC_C_G + BRANCH_WIDTH
SRC_GATES = SRC_C_R + GLA_GATE_RANK
PACK_LANES = 256


def _pack_kernel(w_ref, o_ref):
    def copy(dst, src, rows):
        o_ref[dst:dst + rows, :] = w_ref[src:src + rows, :].astype(BF16)

    copy(COL_A, 0, 2 * D_MODEL)
    copy(COL_GATES, SRC_GATES, 3 * D_MODEL)
    copy(COL_C_V, SRC_C_V, BRANCH_WIDTH)
    copy(COL_C_G, SRC_C_G, BRANCH_WIDTH)
    copy(COL_C_Q, SRC_C_Q, GLA_KEY_WIDTH)
    copy(COL_C_K, SRC_C_K, GLA_KEY_WIDTH)
    copy(COL_C_R, SRC_C_R, GLA_GATE_RANK)
    pad0 = COL_C_R + GLA_GATE_RANK
    o_ref[pad0:IN_PACKED, :] = jnp.zeros((IN_PACKED - pad0, o_ref.shape[1]), BF16)


def _pack_w_in(w_in):
    depth, d, total = w_in.shape
    w_t = jnp.swapaxes(w_in, 1, 2)
    return pl.pallas_call(
        _pack_kernel,
        grid=(depth, d // PACK_LANES),
        in_specs=[pl.BlockSpec((None, total, PACK_LANES), lambda l, i: (l, 0, i))],
        out_specs=pl.BlockSpec((None, IN_PACKED, PACK_LANES), lambda l, i: (l, 0, i)),
        out_shape=jax.ShapeDtypeStruct((depth, IN_PACKED, d), BF16),
        compiler_params=pltpu.CompilerParams(
            dimension_semantics=("arbitrary", "arbitrary"),
            vmem_limit_bytes=VMEM_LIMIT_BYTES),
        name="pack_w_in",
    )(w_t)


def kernel(x, w_in, gla_gate_w2, gla_gate_b, hgrn_lb_logits, hgrn_norm_g, gla_norm_g,
           w_up, w_out, ln_g, ln_b):
    batch, seq, d = x.shape
    n = batch * seq
    lb_soft = jax.nn.softmax(hgrn_lb_logits.astype(F32), axis=0)
    lower_bounds = jnp.cumsum(lb_soft, axis=0) - lb_soft[0:1]
    lvl, incl3 = _recurrence_constants()
    hd = np.arange(BRANCH_WIDTH) // HEAD_DIM
    head_ones = jnp.asarray((hd[:, None] == hd[None, :]).astype(np.float32), dtype=BF16)

    w_packed = _pack_w_in(w_in)
    w2_pad = jnp.zeros((DEPTH, GLA_KEY_WIDTH, GLA_KEY_WIDTH), BF16).at[:, :GLA_GATE_RANK].set(
        gla_gate_w2.astype(BF16))
    w_up_bf = w_up.astype(BF16)
    w_out_bf = w_out.astype(BF16)
    rows = lambda a: a.astype(F32)[:, None, :]

    x2d = x.reshape(n, d)
    for layer in range(DEPTH):
        p, zf, r, vt = _inproj(x2d, w_packed, layer)
        p3 = p.reshape(batch, seq, IN_PACKED)
        a_out = _attention(p3, vt)
        h_out, c_out = _recurrences(
            p3, zf.reshape(batch, seq, -1), r.reshape(batch, seq, -1),
            rows(lower_bounds), w2_pad, rows(gla_gate_b), lvl, incl3, layer)
        x2d = _merge(a_out.reshape(n, -1), h_out.reshape(n, -1), c_out.reshape(n, -1),
                     p, x2d, w_up_bf, w_out_bf, head_ones,
                     rows(hgrn_norm_g), rows(gla_norm_g), rows(ln_g), rows(ln_b), layer)
    return x2d.reshape(batch, seq, d)
```
